```python
import math
import jax, jax.numpy as jnp
from jax import lax
import numpy as np

D_MODEL = 1024
BATCH = 16
SEQ = 2048
DEPTH = 4

GRID_W = 64
CTX_LEN = 256

N_EVEN = (DEPTH + 1) // 2
N_ODD = DEPTH // 2
DN_ALPHA = (2 * DEPTH) ** 0.25
DN_BETA = (8 * DEPTH) ** -0.25
LN_EPS = 1e-6
D_FF = 2816

HY_W = D_MODEL // 2
HY_ORDER = 2
HY_IN = (HY_ORDER + 1) * HY_W
HY_EMB = 33
HY_FO = 64
HY_TARGET = 1e-2
HY_FAST = 0.3
HY_SLOW = 1.5

RW_W = D_MODEL - HY_W
RW_N = 64
RW_H = RW_W // RW_N
RW_DECAY_LORA = 64
RW_A_LORA = 64
RW_GATE_LORA = 128
RW_GN_EPS = 64e-5
RW_IN = 3 * RW_W + 2 * RW_DECAY_LORA + 2 * RW_A_LORA + RW_GATE_LORA
EV_IN = HY_IN + RW_IN

GLA_H = 4
GLA_DK = D_MODEL // 2
GLA_DV = D_MODEL
GLA_HK = GLA_DK // GLA_H
GLA_HV = GLA_DV // GLA_H
GLA_GATE_LORA = 16
GLA_NORMALIZER = 16.0
GLA_CHUNK = 64
GLA_EPS = 1e-5
OD_IN = 2 * GLA_DK + 2 * GLA_DV + 2 * GLA_GATE_LORA

kernel_name = "hyena_rwkv7_gla_prefix_trunk"


def split_sizes(z, sizes):
    idx = [sum(sizes[:i + 1]) for i in range(len(sizes) - 1)]
    return jnp.split(z, idx, axis=-1)


def layer_norm(h, g, b):
    hf = h.astype(jnp.float32)
    mu = jnp.mean(hf, -1, keepdims=True)
    var = jnp.mean(jnp.square(hf - mu), -1, keepdims=True)
    return ((hf - mu) * lax.rsqrt(var + LN_EPS)).astype(h.dtype) * g + b


def modulate(h, shift, scale):
    return h * (1 + scale) + shift


def post_norm_residual(h, y, gate, weight, g, b):
    return layer_norm(DN_ALPHA * h + weight * gate * y, g, b)


def swiglu(h, wg, wu, wd):
    return (jax.nn.silu(h @ wg) * (h @ wu)) @ wd


def neighbours(z):
    zp = jnp.pad(z, ((0, 0), (1, 1), (0, 0)))
    return zp[:, :-2], zp[:, 2:]


def conv3_centered(z, w, b):
    prev, nxt = neighbours(z)
    return prev * w[0] + z * w[1] + nxt * w[2] + b


def raster_to_columns(z):
    B, T = z.shape[:2]
    rows = T // GRID_W
    return z.reshape(B, rows, GRID_W, *z.shape[2:]).swapaxes(1, 2).reshape(z.shape)


def columns_to_raster(z):
    B, T = z.shape[:2]
    rows = T // GRID_W
    return z.reshape(B, GRID_W, rows, *z.shape[2:]).swapaxes(1, 2).reshape(z.shape)


def hyena_filters(L, w1, b1, f1, w2, b2, f2, w3):
    f32 = jnp.float32
    t = jnp.linspace(0.0, 1.0, L, dtype=f32)[:, None]
    bands = (HY_EMB - 1) // 2
    w = 2.0 * math.pi * jnp.arange(L, dtype=f32)[:, None] / L
    f = jnp.linspace(1e-4, bands - 1, bands, dtype=f32)[None, :]
    z = jnp.concatenate([t, jnp.cos(f * w), -jnp.sin(f * w)], axis=-1)
    h = jnp.sin(f1 * (z @ w1 + b1))
    h = jnp.sin(f2 * (h @ w2 + b2))
    h = (h @ w3).astype(f32).reshape(L, HY_ORDER, 2, HY_W)
    deltas = jnp.linspace(math.log(HY_TARGET) / HY_SLOW, math.log(HY_TARGET) / HY_FAST, HY_W, dtype=f32)
    window = jnp.exp(-t * jnp.abs(deltas))
    return h * window[:, None, None, :]


def bidir_long_conv(u, h_fwd, h_bwd, skip):
    L = u.shape[1]
    k = jnp.concatenate([h_fwd, jnp.zeros_like(h_fwd[:1]), h_bwd[:0:-1]], axis=0)
    uf = u.astype(jnp.float32)
    y = jnp.fft.irfft(jnp.fft.rfft(uf, n=2 * L, axis=1) * jnp.fft.rfft(k, n=2 * L, axis=0)[None],
                      n=2 * L, axis=1)[:, :L]
    return y + uf * skip


def hyena_segment(zh, conv_w, conv_b, filt, skip):
    zh = conv3_centered(zh, conv_w, conv_b)
    v, x1, x2 = jnp.split(zh, 3, axis=-1)
    y = x1.astype(jnp.float32) * bidir_long_conv(v, filt[:, 0, 0], filt[:, 0, 1], skip[0])
    y = x2.astype(jnp.float32) * bidir_long_conv(y, filt[:, 1, 0], filt[:, 1, 1], skip[1])
    return y.astype(zh.dtype)


def rwkv_prepare(zr, mu, w0, w_up, a0, a_up, g_up, k_k, k_a):
    f32 = jnp.float32
    B, T, _ = zr.shape
    prev, nxt = neighbours(zr)
    zr = zr + mu * (0.5 * (prev + nxt) - zr)
    r, k, v, wdf, wdb, adf, adb, gd = split_sizes(
        zr, [RW_W, RW_W, RW_W, RW_DECAY_LORA, RW_DECAY_LORA, RW_A_LORA, RW_A_LORA, RW_GATE_LORA])
    heads = lambda t: t.astype(f32).reshape(B, T, RW_H, RW_N)
    g = jax.nn.sigmoid(gd) @ g_up
    kk = heads(k * k_k)
    kk = kk / jnp.maximum(jnp.sqrt(jnp.sum(kk * kk, -1, keepdims=True)), 1e-12)
    dirs = []
    for d, (wd, ad) in enumerate(((wdf, adf), (wdb, adb))):
        w_log = -jax.nn.softplus(-(w0[d] + jnp.tanh(wd) @ w_up[d]).astype(f32)) - 0.5
        decay = jnp.exp(-jnp.exp(w_log))
        a = jax.nn.sigmoid((a0[d] + ad @ a_up[d]).astype(f32))
        kd = k.astype(f32) * (1 + (a - 1) * k_a)
        dirs.append((heads(decay), heads(kd), heads(a)))
    return heads(r), heads(v), kk, g, dirs


def rwkv7_scan(r, decay, k, v, kk, a, s0, reverse):
    def step(S, inp):
        r_t, w_t, k_t, v_t, kk_t, a_t = inp
        sa = jnp.einsum('bhvk,bhk->bhv', S, -kk_t)
        S = (S * w_t[:, :, None, :] + sa[..., None] * (kk_t * a_t)[:, :, None, :]
             + v_t[..., None] * k_t[:, :, None, :])
        return S, jnp.einsum('bhvk,bhk->bhv', S, r_t)
    xs = tuple(jnp.moveaxis(t, 1, 0) for t in (r, decay, k, v, kk, a))
    S, out = lax.scan(step, s0, xs, reverse=reverse)
    return jnp.moveaxis(out, 0, 1), S


def rwkv_bidir(prep, inits):
    r, v, kk, _, dirs = prep
    o_sum, k_sum, finals = 0.0, 0.0, []
    for d, ((decay, kd, a), s0) in enumerate(zip(dirs, inits)):
        o, S = rwkv7_scan(r, decay, kd, v, kk, a, s0, reverse=(d == 1))
        o_sum = o_sum + o
        k_sum = k_sum + kd
        finals.append(S)
    return o_sum, k_sum, tuple(finals)


def rwkv_readout(o, prep, k_sum, r_k, gn_g, gn_b):
    r, v, _, g, _ = prep
    B, T = o.shape[:2]
    mu = jnp.mean(o, -1, keepdims=True)
    var = jnp.mean(jnp.square(o - mu), -1, keepdims=True)
    y = ((o - mu) * lax.rsqrt(var + RW_GN_EPS)).reshape(B, T, RW_W) * gn_g + gn_b
    bonus = jnp.sum(r * k_sum * r_k, -1, keepdims=True) * v
    return (y + bonus.reshape(B, T, RW_W)) * g


def even_mixer(u_ctx, u_lat, w_in, w_out, conv_w, conv_b, f_w1, f_b1, f_freq1, f_w2, f_b2, f_freq2, f_w3,
               skip, mu, w0, w_up, a0, a_up, g_up, k_k, k_a, r_k, gn_g, gn_b, ctx_out):
    z_ctx, z_lat = u_ctx @ w_in, u_lat @ w_in

    def hyena(z):
        filt = hyena_filters(z.shape[1], f_w1, f_b1, f_freq1, f_w2, f_b2, f_freq2, f_w3)
        return hyena_segment(z[..., :HY_IN], conv_w, conv_b, filt, skip)

    p_ctx = rwkv_prepare(z_ctx[..., HY_IN:], mu, w0, w_up, a0, a_up, g_up, k_k, k_a)
    p_lat = rwkv_prepare(z_lat[..., HY_IN:], mu, w0, w_up, a0, a_up, g_up, k_k, k_a)
    s0 = jnp.zeros((z_ctx.shape[0], RW_H, RW_N, RW_N), jnp.float32)
    o_ctx, ks_ctx, fin_ctx = rwkv_bidir(p_ctx, (s0, s0))
    o_lat, ks_lat, _ = rwkv_bidir(p_lat, fin_ctx)
    y_lat = jnp.concatenate([hyena(z_lat).astype(u_lat.dtype),
                             rwkv_readout(o_lat, p_lat, ks_lat, r_k, gn_g, gn_b).astype(u_lat.dtype)], -1) @ w_out
    y_ctx = None
    if ctx_out:
        y_ctx = jnp.concatenate([hyena(z_ctx).astype(u_ctx.dtype),
                                 rwkv_readout(o_ctx, p_ctx, ks_ctx, r_k, gn_g, gn_b).astype(u_ctx.dtype)], -1) @ w_out
    return y_ctx, y_lat


def gla_prepare(z, g_up, g_b):
    f32 = jnp.float32
    B, T, _ = z.shape
    q, k, v, gdf, gdb, og = split_sizes(z, [GLA_DK, GLA_DK, GLA_DV, GLA_GATE_LORA, GLA_GATE_LORA, GLA_DV])
    logs = [(jax.nn.log_sigmoid((gd @ g_up[d] + g_b[d]).astype(f32)) / GLA_NORMALIZER).reshape(B, T, GLA_H, GLA_HK)
            for d, gd in enumerate((gdf, gdb))]
    q = q.astype(f32).reshape(B, T, GLA_H, GLA_HK) * GLA_HK ** -0.5
    k = k.astype(f32).reshape(B, T, GLA_H, GLA_HK)
    v = v.astype(f32).reshape(B, T, GLA_H, GLA_HV)
    return q, k, v, logs, og


def gla_chunked(q, k, v, g, s0):
    B, T, H, K = q.shape
    V = v.shape[-1]
    C = GLA_CHUNK
    n = T // C
    ch = lambda t: t.reshape(B, n, C, H, t.shape[-1]).transpose(0, 1, 3, 2, 4)
    q, k, v, g = ch(q), ch(k), ch(v), ch(g)
    b = jnp.cumsum(g, axis=3)
    b_ref = b[:, :, :, C // 2:C // 2 + 1]
    b_last = b[:, :, :, -1:]
    scores = jnp.einsum('bnhck,bnhsk->bnhcs', q * jnp.exp(b - b_ref), k * jnp.exp(b_ref - b))
    lower = jnp.tril(jnp.ones((C, C), bool))
    o_intra = jnp.einsum('bnhcs,bnhsv->bnhcv', jnp.where(lower, scores, 0.0), v)
    q_in = q * jnp.exp(b)
    k_out = k * jnp.exp(b_last - b)
    d_last = jnp.exp(b_last[:, :, :, 0])

    def step(S, inp):
        qc, kc, vc, dc = inp
        o = jnp.einsum('bhck,bhkv->bhcv', qc, S)
        S = S * dc[..., None] + jnp.einsum('bhck,bhcv->bhkv', kc, vc)
        return S, o
    S, o_inter = lax.scan(step, s0, tuple(jnp.moveaxis(t, 1, 0) for t in (q_in, k_out, v, d_last)))
    o = o_intra + jnp.moveaxis(o_inter, 0, 1)
    return o.transpose(0, 1, 3, 2, 4).reshape(B, T, H, V), S


def gla_bidir(q, k, v, logs, inits):
    flip = lambda t: jnp.flip(t, axis=1)
    o_f, S_f = gla_chunked(q, k, v, logs[0], inits[0])
    o_b, S_b = gla_chunked(flip(q), flip(k), flip(v), flip(logs[1]), inits[1])
    return o_f + flip(o_b), (S_f, S_b)


def gla_readout(o, og, norm_g):
    B, T = o.shape[:2]
    y = o * lax.rsqrt(jnp.mean(jnp.square(o), -1, keepdims=True) + GLA_EPS) * norm_g
    return y.reshape(B, T, GLA_DV) * jax.nn.silu(og.astype(jnp.float32))


def odd_mixer(u_ctx, u_lat, w_in, w_out, g_up, g_b, norm_g, ctx_out):
    qc, kc, vc, lc, ogc = gla_prepare(u_ctx @ w_in, g_up, g_b)
    ql, kl, vl, ll, ogl = gla_prepare(u_lat @ w_in, g_up, g_b)
    s0 = jnp.zeros((u_ctx.shape[0], GLA_H, GLA_HK, GLA_HV), jnp.float32)
    o_ctx, fin_ctx = gla_bidir(qc, kc, vc, lc, (s0, s0))
    o_lat, _ = gla_bidir(raster_to_columns(ql), raster_to_columns(kl), raster_to_columns(vl),
                         [raster_to_columns(t) for t in ll], fin_ctx)
    o_lat = columns_to_raster(o_lat)
    y_lat = gla_readout(o_lat, ogl, norm_g).astype(u_lat.dtype) @ w_out
    y_ctx = None
    if ctx_out:
        y_ctx = gla_readout(o_ctx, ogc, norm_g).astype(u_ctx.dtype) @ w_out
    return y_ctx, y_lat


def setup_inputs(seed: int = 0) -> dict:
    key = jax.random.key(seed)
    ks = iter(jax.random.split(key, 48))
    f32 = jnp.float32
    D, F, NE, NO = D_MODEL, D_FF, N_EVEN, N_ODD

    def nrm(shape, scale):
        return scale * jax.random.normal(next(ks), shape, f32)

    def near(shape, center, spread):
        return center + spread * jax.random.normal(next(ks), shape, f32)

    return {
        "x": nrm((BATCH, SEQ, D), 1.0),
        "c": nrm((BATCH, D), 1.0),
        "ctx": nrm((BATCH, CTX_LEN, D), 1.0),
        "c_ctx": nrm((D,), 1.0),
        "ada_w": nrm((DEPTH, D, 9 * D), 0.5 * D ** -0.5),
        "ada_b": nrm((DEPTH, 9 * D), 0.02),
        "ln_g": near((DEPTH, 3, D), 1.0, 0.02),
        "ln_b": nrm((DEPTH, 3, D), 0.02),
        "ffn_wg": nrm((DEPTH, 2, D, F), D ** -0.5),
        "ffn_wu": nrm((DEPTH, 2, D, F), D ** -0.5),
        "ffn_wd": nrm((DEPTH, 2, F, D), DN_BETA * F ** -0.5),
        "ev_w_in": nrm((NE, D, EV_IN), D ** -0.5),
        "ev_w_out": nrm((NE, D, D), DN_BETA * D ** -0.5),
        "hy_conv_w": nrm((NE, 3, HY_IN), 3 ** -0.5),
        "hy_conv_b": nrm((NE, HY_IN), 0.02),
        "hy_f_w1": nrm((NE, HY_EMB, HY_FO), HY_EMB ** -0.5),
        "hy_f_b1": nrm((NE, HY_FO), 0.1),
        "hy_f_freq1": near((NE, HY_FO), 1.0, 0.02),
        "hy_f_w2": nrm((NE, HY_FO, HY_FO), HY_FO ** -0.5),
        "hy_f_b2": nrm((NE, HY_FO), 0.1),
        "hy_f_freq2": near((NE, HY_FO), 1.0, 0.02),
        "hy_f_w3": nrm((NE, HY_FO, 2 * HY_ORDER * HY_W), 0.1 * HY_FO ** -0.5),
        "hy_skip": nrm((NE, HY_ORDER, HY_W), 0.5),
        "rw_mu": jax.random.uniform(next(ks), (NE, RW_IN), f32),
        "rw_w0": jax.random.uniform(next(ks), (NE, 2, RW_W), f32, -2.0, 1.0),
        "rw_w_up": nrm((NE, 2, RW_DECAY_LORA, RW_W), 0.5 * RW_DECAY_LORA ** -0.5),
        "rw_a0": nrm((NE, 2, RW_W), 0.5),
        "rw_a_up": nrm((NE, 2, RW_A_LORA, RW_W), 0.5 * RW_A_LORA ** -0.5),
        "rw_g_up": nrm((NE, RW_GATE_LORA, RW_W), RW_GATE_LORA ** -0.5),
        "rw_k_k": near((NE, RW_W), 0.85, 0.02),
        "rw_k_a": near((NE, RW_W), 1.0, 0.02),
        "rw_r_k": nrm((NE, RW_H, RW_N), 0.1),
        "rw_gn_g": near((NE, RW_W), 1.0, 0.02),
        "rw_gn_b": nrm((NE, RW_W), 0.02),
        "od_w_in": nrm((NO, D, OD_IN), D ** -0.5),
        "od_w_out": nrm((NO, GLA_DV, D), DN_BETA * GLA_DV ** -0.5),
        "gla_g_up": nrm((NO, 2, GLA_GATE_LORA, GLA_DK), GLA_GATE_LORA ** -0.5),
        "gla_g_b": nrm((NO, 2, GLA_DK), 0.5),
        "gla_norm_g": near((NO, GLA_HV), 1.0, 0.02),
    }


def reference(x, c, ctx, c_ctx, ada_w, ada_b, ln_g, ln_b, ffn_wg, ffn_wu, ffn_wd, ev_w_in, ev_w_out,
              hy_conv_w, hy_conv_b, hy_f_w1, hy_f_b1, hy_f_freq1, hy_f_w2, hy_f_b2, hy_f_freq2, hy_f_w3, hy_skip,
              rw_mu, rw_w0, rw_w_up, rw_a0, rw_a_up, rw_g_up, rw_k_k, rw_k_a, rw_r_k, rw_gn_g, rw_gn_b,
              od_w_in, od_w_out, gla_g_up, gla_g_b, gla_norm_g):
    h_lat, h_ctx = x, ctx
    s_lat = jax.nn.silu(c)[:, None, :]
    s_ctx = jax.nn.silu(c_ctx)[None, None, :]
    for l in range(DEPTH):
        last = l == DEPTH - 1
        m_lat = jnp.split(s_lat @ ada_w[l] + ada_b[l], 9, axis=-1)
        m_ctx = jnp.split(s_ctx @ ada_w[l] + ada_b[l], 9, axis=-1)

        h_lat = post_norm_residual(
            h_lat, swiglu(modulate(h_lat, m_lat[0], m_lat[1]), ffn_wg[l, 0], ffn_wu[l, 0], ffn_wd[l, 0]),
            m_lat[2], 0.5, ln_g[l, 0], ln_b[l, 0])
        h_ctx = post_norm_residual(
            h_ctx, swiglu(modulate(h_ctx, m_ctx[0], m_ctx[1]), ffn_wg[l, 0], ffn_wu[l, 0], ffn_wd[l, 0]),
            m_ctx[2], 0.5, ln_g[l, 0], ln_b[l, 0])

        u_lat = modulate(h_lat, m_lat[3], m_lat[4])
        u_ctx = modulate(h_ctx, m_ctx[3], m_ctx[4])
        if l % 2 == 0:
            e = l // 2
            y_ctx, y_lat = even_mixer(
                u_ctx, u_lat, ev_w_in[e], ev_w_out[e], hy_conv_w[e], hy_conv_b[e], hy_f_w1[e], hy_f_b1[e],
                hy_f_freq1[e], hy_f_w2[e], hy_f_b2[e], hy_f_freq2[e], hy_f_w3[e], hy_skip[e], rw_mu[e], rw_w0[e],
                rw_w_up[e], rw_a0[e], rw_a_up[e], rw_g_up[e], rw_k_k[e], rw_k_a[e], rw_r_k[e], rw_gn_g[e], rw_gn_b[e],
                ctx_out=not last)
        else:
            o = l // 2
            y_ctx, y_lat = odd_mixer(u_ctx, u_lat, od_w_in[o], od_w_out[o], gla_g_up[o], gla_g_b[o],
                                     gla_norm_g[o], ctx_out=not last)
        h_lat = post_norm_residual(h_lat, y_lat, m_lat[5], 1.0, ln_g[l, 1], ln_b[l, 1])

        h_lat = post_norm_residual(
            h_lat, swiglu(modulate(h_lat, m_lat[6], m_lat[7]), ffn_wg[l, 1], ffn_wu[l, 1], ffn_wd[l, 1]),
            m_lat[8], 0.5, ln_g[l, 2], ln_b[l, 2])
        if not last:
            h_ctx = post_norm_residual(h_ctx, y_ctx, m_ctx[5], 1.0, ln_g[l, 1], ln_b[l, 1])
            h_ctx = post_norm_residual(
                h_ctx, swiglu(modulate(h_ctx, m_ctx[6], m_ctx[7]), ffn_wg[l, 1], ffn_wu[l, 1], ffn_wd[l, 1]),
                m_ctx[8], 0.5, ln_g[l, 2], ln_b[l, 2])
    return h_lat
```

```python
import functools
import math

import jax
import jax.numpy as jnp
from jax import lax
from jax.experimental import pallas as pl
from jax.experimental.pallas import tpu as pltpu

F32 = jnp.float32
BF16 = jnp.bfloat16
HIGHEST = lax.Precision.HIGHEST

D_MODEL = 1024
BATCH = 16
SEQ = 2048
DEPTH = 4
GRID_W = 64
CTX_LEN = 256
DN_ALPHA = (2 * DEPTH) ** 0.25
LN_EPS = 1e-6
D_FF = 2816

HY_W = D_MODEL // 2
HY_ORDER = 2
HY_IN = (HY_ORDER + 1) * HY_W
HY_EMB = 33
HY_FO = 64
HY_TARGET = 1e-2
HY_FAST = 0.3
HY_SLOW = 1.5

RW_W = D_MODEL - HY_W
RW_N = 64
RW_H = RW_W // RW_N
RW_DECAY_LORA = 64
RW_A_LORA = 64
RW_GATE_LORA = 128
RW_GN_EPS = 64e-5
RW_IN = 3 * RW_W + 2 * RW_DECAY_LORA + 2 * RW_A_LORA + RW_GATE_LORA
EV_IN = HY_IN + RW_IN

GLA_H = 4
GLA_DK = D_MODEL // 2
GLA_DV = D_MODEL
GLA_HK = GLA_DK // GLA_H
GLA_HV = GLA_DV // GLA_H
GLA_GATE_LORA = 16
GLA_NORMALIZER = 16.0
GLA_CHUNK = 64
GLA_EPS = 1e-5

LANES = 128
SUBLANES = 8

NTOK = SEQ + CTX_LEN
TM = 256
NT = NTOK // TM
NT_LAT = SEQ // TM
MOD_ROWS = 24
TF = 256
NF = D_FF // TF
SCAN_TB = 32
BH = BATCH * RW_H
HY_FB = 256
HY_CT = 256
VMEM_LIMIT = 56 * 1024 * 1024


def _cparams(n_axes):
    return pltpu.CompilerParams(dimension_semantics=("arbitrary",) * n_axes,
                                vmem_limit_bytes=VMEM_LIMIT)


def _layer_norm_rows(r, g, b):
    mu = jnp.mean(r, axis=-1, keepdims=True)
    xc = r - mu
    var = jnp.mean(xc * xc, axis=-1, keepdims=True)
    return xc * lax.rsqrt(var + LN_EPS) * g + b


def _silu(x):
    return x * jax.nn.sigmoid(x)


def _split_bf16(x):
    hi = x.astype(BF16)
    lo = (x - hi.astype(F32)).astype(BF16)
    return hi, lo


def _dot(a, b):
    return jnp.dot(a, b, preferred_element_type=F32)


def _dot_f32(a, b):
    return jnp.dot(a, b, preferred_element_type=F32, precision=HIGHEST)


def _dot3(a_hi, a_lo, b_hi, b_lo):
    return _dot(a_hi, b_hi) + (_dot(a_lo, b_hi) + _dot(a_hi, b_lo))


def _mod_spec(layer):
    def index(b, t):
        return (layer, jnp.where(t >= NT_LAT, BATCH, b), 0, 0)
    return pl.BlockSpec((None, None, 9, D_MODEL), index)


def _tok_spec(width, col=0):
    return pl.BlockSpec((None, TM, width), lambda b, t: (b, t, col))


def _const_spec(shape):
    nd = len(shape)
    return pl.BlockSpec(shape, lambda b, t: (0,) * nd)


def _ada_kernel(s_ref, w_ref, b_ref, o_ref):
    s = _silu(s_ref[...])
    o_ref[...] = _dot_f32(s, w_ref[...]) + b_ref[...]


def _ada_call(s, ada_w, ada_b):
    tn = 2304
    n_blk = 9 * D_MODEL // tn
    return pl.pallas_call(
        _ada_kernel,
        grid=(DEPTH, n_blk),
        in_specs=[
            pl.BlockSpec((MOD_ROWS, D_MODEL), lambda l, j: (0, 0)),
            pl.BlockSpec((None, D_MODEL, tn), lambda l, j: (l, 0, j)),
            pl.BlockSpec((None, 1, tn), lambda l, j: (l, 0, j)),
        ],
        out_specs=pl.BlockSpec((None, MOD_ROWS, tn), lambda l, j: (l, 0, j)),
        out_shape=jax.ShapeDtypeStruct((DEPTH, MOD_ROWS, 9 * D_MODEL), F32),
        compiler_params=_cparams(2),
        name="ada_mod",
    )(s, ada_w, ada_b.reshape(DEPTH, 1, 9 * D_MODEL))


def _ffn_kernel(h_ref, m_ref, wg_ref, wu_ref, wd_ref, g_ref, b_ref, o_ref, *, mi):
    h = h_ref[...]
    shift = m_ref[mi:mi + 1, :]
    scale = m_ref[mi + 1:mi + 2, :]
    gate = m_ref[mi + 2:mi + 3, :]
    hm = (h * (1.0 + scale) + shift).astype(BF16)
    acc = jnp.zeros((TM, D_MODEL), F32)
    for j in range(NF):
        a = _silu(_dot(hm, wg_ref[j])) * _dot(hm, wu_ref[j])
        acc = acc + _dot(a.astype(BF16), wd_ref[j])
    r = DN_ALPHA * h + (0.5 * gate) * acc
    o_ref[...] = _layer_norm_rows(r, g_ref[...], b_ref[...])


def _ffn_call(h, mods, layer, mi, wg, wu, wd, ln_g, ln_b):
    return pl.pallas_call(
        functools.partial(_ffn_kernel, mi=mi),
        grid=(BATCH, NT),
        in_specs=[
            _tok_spec(D_MODEL),
            _mod_spec(layer),
            _const_spec((NF, D_MODEL, TF)),
            _const_spec((NF, D_MODEL, TF)),
            _const_spec((NF, TF, D_MODEL)),
            _const_spec((1, D_MODEL)),
            _const_spec((1, D_MODEL)),
        ],
        out_specs=_tok_spec(D_MODEL),
        out_shape=jax.ShapeDtypeStruct((BATCH, NTOK, D_MODEL), F32),
        compiler_params=_cparams(2),
        name="ffn",
    )(h, mods, wg, wu, wd, ln_g, ln_b)


def _ffn_weights(wg, wu, wd):
    wg = wg.astype(BF16).reshape(D_MODEL, NF, TF).transpose(1, 0, 2)
    wu = wu.astype(BF16).reshape(D_MODEL, NF, TF).transpose(1, 0, 2)
    wd = wd.astype(BF16).reshape(NF, TF, D_MODEL)
    return wg, wu, wd


def _inproj_kernel(h_ref, m_ref, w_ref, *o_refs, widths):
    u = (h_ref[...] * (1.0 + m_ref[4:5, :]) + m_ref[3:4, :]).astype(BF16)
    off = 0
    for o_ref, wd in zip(o_refs, widths):
        o_ref[...] = _dot(u, w_ref[:, off:off + wd])
        off += wd


def _inproj_call(h, mods, layer, w, widths):
    n_in = sum(widths)
    return pl.pallas_call(
        functools.partial(_inproj_kernel, widths=widths),
        grid=(BATCH, NT),
        in_specs=[_tok_spec(D_MODEL), _mod_spec(layer), _const_spec((D_MODEL, n_in))],
        out_specs=[_tok_spec(wd) for wd in widths],
        out_shape=[jax.ShapeDtypeStruct((BATCH, NTOK, wd), F32) for wd in widths],
        compiler_params=_cparams(2),
        name="mixer_in",
    )(h, mods, w)


def _hyena_feature_table(L):
    t = jnp.linspace(0.0, 1.0, L, dtype=F32)[:, None]
    bands = (HY_EMB - 1) // 2
    w = 2.0 * math.pi * jnp.arange(L, dtype=F32)[:, None] / L
    f = jnp.linspace(1e-4, bands - 1, bands, dtype=F32)[None, :]
    z = jnp.concatenate([t, jnp.cos(f * w), -jnp.sin(f * w)], axis=-1)
    deltas = jnp.linspace(math.log(HY_TARGET) / HY_SLOW, math.log(HY_TARGET) / HY_FAST, HY_W, dtype=F32)
    return jnp.pad(z, ((0, 0), (0, LANES - HY_EMB))), t, jnp.abs(deltas)[None, :]


def _hy_filter_kernel(z_ref, t_ref, d_ref, w1_ref, b1_ref, f1_ref, w2_ref, b2_ref, f2_ref, w3_ref,
                      s_ref, dd_ref):
    h = jnp.sin(f1_ref[...] * (_dot_f32(z_ref[...], w1_ref[...]) + b1_ref[...]))
    h = jnp.sin(f2_ref[...] * (_dot_f32(h, w2_ref[...]) + b2_ref[...]))
    window = jnp.exp(-t_ref[...] * d_ref[...])
    rows = lax.broadcasted_iota(jnp.int32, window.shape, 0)
    for o in range(HY_ORDER):
        base = o * 2 * HY_W
        h_fwd = _dot_f32(h, w3_ref[:, base:base + HY_W]) * window
        h_bwd = _dot_f32(h, w3_ref[:, base + HY_W:base + 2 * HY_W]) * window
        h_bwd = jnp.where(rows == 0, 0.0, h_bwd)
        s_ref[o] = h_fwd + h_bwd
        dd_ref[o] = h_fwd - h_bwd


def _hy_filter_call(L, w1, b1, f1, w2, b2, f2, w3):
    z, t, deltas = _hyena_feature_table(L)
    w1p = jnp.pad(w1, ((0, LANES - HY_EMB), (0, 0)))
    shp = jax.ShapeDtypeStruct((HY_ORDER, L, HY_W), F32)
    return pl.pallas_call(
        _hy_filter_kernel,
        out_shape=[shp, shp],
        compiler_params=pltpu.CompilerParams(vmem_limit_bytes=VMEM_LIMIT),
        name="hyena_filter",
    )(z, t, deltas, w1p, b1[None, :], f1[None, :], w2, b2[None, :], f2[None, :], w3)


def _dft_tables(L):
    n = 2 * L
    fb = min(HY_FB, L)
    nfb = L // fb
    f = jnp.arange(L, dtype=jnp.int32)[:, None]
    t = jnp.arange(L, dtype=jnp.int32)[None, :]
    ang = ((f * t) % n).astype(F32) * (2.0 * math.pi / n)
    cos, sin = jnp.cos(ang), jnp.sin(ang)
    alt = jnp.where(t % 2 == 0, 1.0, -1.0).astype(F32)
    g_re = cos
    g_im = jnp.where(f == 0, alt, -sin)
    scale = jnp.where(f == 0, 1.0 / n, 2.0 / n).astype(F32)
    i_re = (scale * cos).T
    i_im = jnp.where(f == 0, alt / n, -scale * sin).T
    fwd = jnp.concatenate([g_re.reshape(nfb, fb, L), g_im.reshape(nfb, fb, L)], axis=1)
    inv = jnp.concatenate([i_re.reshape(L, nfb, fb).transpose(1, 0, 2),
                           i_im.reshape(L, nfb, fb).transpose(1, 0, 2)], axis=2)
    return _split_bf16(fwd), _split_bf16(inv)


def _hy_spectrum_kernel(gh_ref, gl_ref, s_ref, d_ref, k_ref, *, fb):
    j = pl.program_id(1)
    s_hi, s_lo = _split_bf16(s_ref[...])
    d_hi, d_lo = _split_bf16(d_ref[...])
    k_re = _dot3(gh_ref[:fb, :], gl_ref[:fb, :], s_hi, s_lo)
    k_im = _dot3(gh_ref[fb:, :], gl_ref[fb:, :], d_hi, d_lo)
    k_ny = _dot3(gh_ref[fb:, :], gl_ref[fb:, :], s_hi, s_lo)
    rows = lax.broadcasted_iota(jnp.int32, k_im.shape, 0)
    k_ref[0] = k_re
    k_ref[1] = jnp.where((rows == 0) & (j == 0), k_ny, k_im)


def _hy_spectrum_call(L, fwd_tab, filt_s, filt_d):
    fb = min(HY_FB, L)
    nfb = L // fb
    gh, gl = fwd_tab
    return pl.pallas_call(
        functools.partial(_hy_spectrum_kernel, fb=fb),
        grid=(HY_ORDER, nfb),
        in_specs=[
            pl.BlockSpec((None, 2 * fb, L), lambda o, j: (j, 0, 0)),
            pl.BlockSpec((None, 2 * fb, L), lambda o, j: (j, 0, 0)),
            pl.BlockSpec((None, L, HY_W), lambda o, j: (o, 0, 0)),
            pl.BlockSpec((None, L, HY_W), lambda o, j: (o, 0, 0)),
        ],
        out_specs=pl.BlockSpec((None, 2, fb, HY_W), lambda o, j: (o, 0, j, 0)),
        out_shape=jax.ShapeDtypeStruct((HY_ORDER, 2, L, HY_W), F32),
        compiler_params=_cparams(2),
        name="hyena_spectrum",
    )(gh, gl, filt_s, filt_d)


def _conv3_rows(z, w, b):
    n = z.shape[0]
    rows = lax.broadcasted_iota(jnp.int32, z.shape, 0)
    prev = jnp.where(rows == 0, 0.0, pltpu.roll(z, 1, 0))
    nxt = jnp.where(rows == n - 1, 0.0, pltpu.roll(z, n - 1, 0))
    return prev * w[0:1, :] + z * w[1:2, :] + nxt * w[2:3, :] + b


def _hy_conv_kernel(zv_ref, z1_ref, z2_ref, cwv_ref, cw1_ref, cw2_ref, cbv_ref, cb1_ref, cb2_ref,
                    skip_ref, k_ref, fh_ref, fl_ref, ih_ref, il_ref, o_ref,
                    cur_ref, hi_ref, lo_ref, acc_ref, *, fb, nfb):
    p = pl.program_id(2)
    j = pl.program_id(3)

    @pl.when((p == 0) & (j == 0))
    def _():
        cur_ref[...] = _conv3_rows(zv_ref[...], cwv_ref[...], cbv_ref[...])

    @pl.when(j == 0)
    def _():
        hi, lo = _split_bf16(cur_ref[...])
        hi_ref[...] = hi
        lo_ref[...] = lo
        acc_ref[...] = jnp.zeros_like(acc_ref)

    spec = _dot3(fh_ref[...], fl_ref[...], hi_ref[...], lo_ref[...])
    v_re, v_im = spec[:fb], spec[fb:]
    k_re, k_im = k_ref[0], k_ref[1]
    rows = lax.broadcasted_iota(jnp.int32, v_re.shape, 0)
    packed = (rows == 0) & (j == 0)
    y_re = jnp.where(packed, v_re * k_re, v_re * k_re - v_im * k_im)
    y_im = jnp.where(packed, v_im * k_im, v_re * k_im + v_im * k_re)
    y_hi, y_lo = _split_bf16(jnp.concatenate([y_re, y_im], axis=0))
    acc_ref[...] += _dot3(ih_ref[...], il_ref[...], y_hi, y_lo)

    @pl.when((j == nfb - 1) & (p == 0))
    def _():
        x1 = _conv3_rows(z1_ref[...], cw1_ref[...], cb1_ref[...])
        cur = cur_ref[...]
        cur_ref[...] = x1 * (acc_ref[...] + skip_ref[0:1, :] * cur)

    @pl.when((j == nfb - 1) & (p == 1))
    def _():
        x2 = _conv3_rows(z2_ref[...], cw2_ref[...], cb2_ref[...])
        cur = cur_ref[...]
        o_ref[...] = x2 * (acc_ref[...] + skip_ref[1:2, :] * cur)


def _hy_conv_call(z_hy, L, row_blk, conv_w, conv_b, skip, spectrum, fwd_tab, inv_tab):
    fb = min(HY_FB, L)
    nfb = L // fb
    ct = HY_CT
    nct = HY_W // ct
    fh, fl = fwd_tab
    ih, il = inv_tab

    def zspec(part):
        return pl.BlockSpec((None, L, ct), lambda b, c, p, j: (b, row_blk, part * nct + c))

    def cspec(rows, part):
        return pl.BlockSpec((rows, ct), lambda b, c, p, j: (0, part * nct + c))

    return pl.pallas_call(
        functools.partial(_hy_conv_kernel, fb=fb, nfb=nfb),
        grid=(BATCH, nct, HY_ORDER, nfb),
        in_specs=[
            zspec(0), zspec(1), zspec(2),
            cspec(3, 0), cspec(3, 1), cspec(3, 2),
            cspec(1, 0), cspec(1, 1), cspec(1, 2),
            pl.BlockSpec((HY_ORDER, ct), lambda b, c, p, j: (0, c)),
            pl.BlockSpec((None, 2, fb, ct), lambda b, c, p, j: (p, 0, j, c)),
            pl.BlockSpec((None, 2 * fb, L), lambda b, c, p, j: (j, 0, 0)),
            pl.BlockSpec((None, 2 * fb, L), lambda b, c, p, j: (j, 0, 0)),
            pl.BlockSpec((None, L, 2 * fb), lambda b, c, p, j: (j, 0, 0)),
            pl.BlockSpec((None, L, 2 * fb), lambda b, c, p, j: (j, 0, 0)),
        ],
        out_specs=pl.BlockSpec((None, L, ct), lambda b, c, p, j: (b, 0, c)),
        out_shape=jax.ShapeDtypeStruct((BATCH, L, HY_W), F32),
        scratch_shapes=[
            pltpu.VMEM((L, ct), F32),
            pltpu.VMEM((L, ct), BF16),
            pltpu.VMEM((L, ct), BF16),
            pltpu.VMEM((L, ct), F32),
        ],
        compiler_params=_cparams(4),
        name="hyena_conv",
    )(z_hy, z_hy, z_hy, conv_w, conv_w, conv_w, conv_b, conv_b, conv_b, skip, spectrum, fh, fl, ih, il)


def _hyena(z_hy, conv_w, conv_b, f_w1, f_b1, f_freq1, f_w2, f_b2, f_freq2, f_w3, skip):
    outs = []
    for L, row_blk in ((SEQ, 0), (CTX_LEN, SEQ // CTX_LEN)):
        fwd_tab, inv_tab = _dft_tables(L)
        filt_s, filt_d = _hy_filter_call(L, f_w1, f_b1, f_freq1, f_w2, f_b2, f_freq2, f_w3)
        spectrum = _hy_spectrum_call(L, fwd_tab, filt_s, filt_d)
        outs.append(_hy_conv_call(z_hy, L, row_blk, conv_w, conv_b[None, :], skip, spectrum, fwd_tab, inv_tab))
    return jnp.concatenate(outs, axis=1)


def _head_sum_matrix(width, head):
    i = jnp.arange(width)[:, None] // head
    j = jnp.arange(width)[None, :] // head
    return (i == j).astype(BF16)


def _group_sum(x, ones_ref):
    hi, lo = _split_bf16(x)
    return _dot(hi, ones_ref[...]) + _dot(lo, ones_ref[...])


def _rw_prep_kernel(z_ref, zp_ref, zn_ref, mu_ref, kk_ref, ka_ref, rk_ref, w0_ref, a0_ref, wup_ref, aup_ref,
                    gup_ref, ones_ref,
                    r_ref, v_ref, nkk_ref, g_ref, bonus_ref, wf_ref, kf_ref, af_ref, wb_ref, kb_ref, ab_ref):
    t = pl.program_id(1)
    z = z_ref[...]
    first = (t == 0) | (t == NT_LAT)
    last = (t == NT_LAT - 1) | (t == NT - 1)
    prev_row = jnp.where(first, 0.0, zp_ref[SUBLANES - 1:SUBLANES, :])
    next_row = jnp.where(last, 0.0, zn_ref[0:1, :])
    rows = lax.broadcasted_iota(jnp.int32, z.shape, 0)
    prev = jnp.where(rows == 0, prev_row, pltpu.roll(z, 1, 0))
    nxt = jnp.where(rows == TM - 1, next_row, pltpu.roll(z, TM - 1, 0))
    z = z + mu_ref[...] * (0.5 * (prev + nxt) - z)

    W = RW_W
    r = z[:, 0:W]
    k = z[:, W:2 * W]
    v = z[:, 2 * W:3 * W]
    o = 3 * W
    w_lora = _dot_f32(jnp.tanh(z[:, o:o + 2 * RW_DECAY_LORA]), wup_ref[...])
    o += 2 * RW_DECAY_LORA
    a_lora = _dot_f32(z[:, o:o + 2 * RW_A_LORA], aup_ref[...])
    o += 2 * RW_A_LORA
    g = _dot_f32(jax.nn.sigmoid(z[:, o:o + RW_GATE_LORA]), gup_ref[...])

    kk = k * kk_ref[...]
    norm = jnp.sqrt(_group_sum(kk * kk, ones_ref))
    kk = kk / jnp.maximum(norm, 1e-12)

    k_sum = jnp.zeros_like(k)
    for d, (w_ref, kd_ref, ad_ref) in enumerate(((wf_ref, kf_ref, af_ref), (wb_ref, kb_ref, ab_ref))):
        y = -(w0_ref[d:d + 1, :] + w_lora[:, d * W:(d + 1) * W])
        softplus = jnp.maximum(y, 0.0) + jnp.log(1.0 + jnp.exp(-jnp.abs(y)))
        w_log = -softplus - 0.5
        a = jax.nn.sigmoid(a0_ref[d:d + 1, :] + a_lora[:, d * W:(d + 1) * W])
        kd = k * (1.0 + (a - 1.0) * ka_ref[...])
        w_ref[...] = jnp.exp(-jnp.exp(w_log))
        kd_ref[...] = kd
        ad_ref[...] = kk * a
        k_sum = k_sum + kd

    r_ref[...] = r
    v_ref[...] = v
    nkk_ref[...] = -kk
    g_ref[...] = g
    bonus_ref[...] = _group_sum(r * k_sum * rk_ref[...], ones_ref) * v


def _rw_prep_call(z_rw, mu, k_k, k_a, r_k, w0, a0, w_up, a_up, g_up):
    W = RW_W
    zero = jnp.zeros((RW_DECAY_LORA, W), F32)
    wup = jnp.concatenate([jnp.concatenate([w_up[0], zero], 1), jnp.concatenate([zero, w_up[1]], 1)], 0)
    aup = jnp.concatenate([jnp.concatenate([a_up[0], zero], 1), jnp.concatenate([zero, a_up[1]], 1)], 0)
    blocks_per_tile = TM // SUBLANES
    n_row_blocks = NTOK // SUBLANES
    out = jax.ShapeDtypeStruct((BATCH, NTOK, W), F32)
    return pl.pallas_call(
        _rw_prep_kernel,
        grid=(BATCH, NT),
        in_specs=[
            _tok_spec(RW_IN),
            pl.BlockSpec((None, SUBLANES, RW_IN),
                         lambda b, t: (b, jnp.maximum(t * blocks_per_tile - 1, 0), 0)),
            pl.BlockSpec((None, SUBLANES, RW_IN),
                         lambda b, t: (b, jnp.minimum((t + 1) * blocks_per_tile, n_row_blocks - 1), 0)),
            _const_spec((1, RW_IN)),
            _const_spec((1, W)), _const_spec((1, W)), _const_spec((1, W)),
            _const_spec((2, W)), _const_spec((2, W)),
            _const_spec((2 * RW_DECAY_LORA, 2 * W)), _const_spec((2 * RW_A_LORA, 2 * W)),
            _const_spec((RW_GATE_LORA, W)),
            _const_spec((W, W)),
        ],
        out_specs=[_tok_spec(W)] * 11,
        out_shape=[out] * 11,
        compiler_params=_cparams(2),
        name="rwkv_prepare",
    )(z_rw, z_rw, z_rw, mu[None, :], k_k[None, :], k_a[None, :], r_k.reshape(1, W), w0, a0, wup, aup, g_up,
      _head_sum_matrix(W, RW_N))


def _rw_scan_kernel(r_ref, v_ref, nkk_ref, w_ref, kd_ref, ka_ref, o_ref, s_ref, *, reverse):
    @pl.when(pl.program_id(0) == 0)
    def _():
        s_ref[...] = jnp.zeros_like(s_ref)

    def step(i, carry):
        t = SCAN_TB - 1 - i if reverse else i
        sa = jnp.zeros((RW_N, BH), F32)
        for k in range(RW_N):
            sa = sa + s_ref[k] * nkk_ref[t, k:k + 1, :]
        vv = v_ref[t]
        out = jnp.zeros((RW_N, BH), F32)
        for k in range(RW_N):
            s_k = (s_ref[k] * w_ref[t, k:k + 1, :] + sa * ka_ref[t, k:k + 1, :]
                   + vv * kd_ref[t, k:k + 1, :])
            s_ref[k] = s_k
            out = out + s_k * r_ref[t, k:k + 1, :]
        o_ref[t] = out
        return carry

    lax.fori_loop(0, SCAN_TB, step, 0)


def _rw_scan_call(r, v, nkk, w, kd, ka, reverse):
    n_lat = SEQ // SCAN_TB
    n_ctx = CTX_LEN // SCAN_TB

    def index(i):
        if reverse:
            blk = jnp.where(i < n_ctx, n_lat + n_ctx - 1 - i, n_lat + n_ctx - 1 - i)
        else:
            blk = jnp.where(i < n_ctx, n_lat + i, i - n_ctx)
        return (blk, 0, 0)

    spec = pl.BlockSpec((SCAN_TB, RW_N, BH), index)
    return pl.pallas_call(
        functools.partial(_rw_scan_kernel, reverse=reverse),
        grid=(n_lat + n_ctx,),
        in_specs=[spec] * 6,
        out_specs=spec,
        out_shape=jax.ShapeDtypeStruct((NTOK, RW_N, BH), F32),
        scratch_shapes=[pltpu.VMEM((RW_N, RW_N, BH), F32)],
        compiler_params=_cparams(1),
        name="rwkv_scan",
    )(r, v, nkk, w, kd, ka)


def _to_scan(x):
    return x.reshape(BATCH, NTOK, RW_H, RW_N).transpose(1, 3, 0, 2).reshape(NTOK, RW_N, BH)


def _from_scan(x):
    return x.reshape(NTOK, RW_N, BATCH, RW_H).transpose(2, 0, 3, 1).reshape(BATCH, NTOK, RW_W)


def _even_out_kernel(h_ref, m_ref, hy_ref, of_ref, ob_ref, bonus_ref, g_ref, gng_ref, gnb_ref, ones_ref,
                     why_ref, wrw_ref, lg_ref, lb_ref, o_ref):
    o = of_ref[...] + ob_ref[...]
    mu = _group_sum(o, ones_ref) * (1.0 / RW_N)
    oc = o - mu
    var = _group_sum(oc * oc, ones_ref) * (1.0 / RW_N)
    y = oc * lax.rsqrt(var + RW_GN_EPS) * gng_ref[...] + gnb_ref[...]
    rw = (y + bonus_ref[...]) * g_ref[...]
    mix = _dot(hy_ref[...].astype(BF16), why_ref[...]) + _dot(rw.astype(BF16), wrw_ref[...])
    r = DN_ALPHA * h_ref[...] + m_ref[5:6, :] * mix
    o_ref[...] = _layer_norm_rows(r, lg_ref[...], lb_ref[...])


def _even_out_call(h, mods, layer, hy, o_f, o_b, bonus, g, gn_g, gn_b, w_out, ln_g, ln_b):
    W = RW_W
    w_out = w_out.astype(BF16)
    return pl.pallas_call(
        _even_out_kernel,
        grid=(BATCH, NT),
        in_specs=[
            _tok_spec(D_MODEL), _mod_spec(layer),
            _tok_spec(HY_W), _tok_spec(W), _tok_spec(W), _tok_spec(W), _tok_spec(W),
            _const_spec((1, W)), _const_spec((1, W)), _const_spec((W, W)),
            _const_spec((HY_W, D_MODEL)), _const_spec((W, D_MODEL)),
            _const_spec((1, D_MODEL)), _const_spec((1, D_MODEL)),
        ],
        out_specs=_tok_spec(D_MODEL),
        out_shape=jax.ShapeDtypeStruct((BATCH, NTOK, D_MODEL), F32),
        compiler_params=_cparams(2),
        name="even_out",
    )(h, mods, hy, o_f, o_b, bonus, g, gn_g[None, :], gn_b[None, :], _head_sum_matrix(W, RW_N),
      w_out[:HY_W], w_out[HY_W:], ln_g, ln_b)


GLA_GPAD = LANES
OD_WIDTHS = (GLA_DK, GLA_DK, GLA_DV, GLA_DV, GLA_GPAD)


def _gla_kernel(q_ref, k_ref, v_ref, gd_ref, gup_ref, gb_ref, o_ref, s_ref, *, reverse):
    C = GLA_CHUNK

    @pl.when(pl.program_id(1) == 0)
    def _():
        s_ref[...] = jnp.zeros_like(s_ref)

    logit = _dot_f32(gd_ref[...], gup_ref[...]) + gb_ref[...]
    log_sig = jnp.minimum(logit, 0.0) - jnp.log(1.0 + jnp.exp(-jnp.abs(logit)))
    g = log_sig * (1.0 / GLA_NORMALIZER)
    ri = lax.broadcasted_iota(jnp.int32, (C, C), 0)
    ci = lax.broadcasted_iota(jnp.int32, (C, C), 1)
    causal = (ci >= ri) if reverse else (ci <= ri)
    b = _dot_f32(causal.astype(F32), g)
    mid, end = (C - 1 - C // 2, 0) if reverse else (C // 2, C - 1)
    b_mid = b[mid:mid + 1, :]
    b_end = b[end:end + 1, :]
    q = q_ref[...] * (GLA_HK ** -0.5)
    k = k_ref[...]
    q_intra = (q * jnp.exp(b - b_mid)).astype(BF16)
    k_intra = (k * jnp.exp(b_mid - b)).astype(BF16)
    q_in = (q * jnp.exp(b)).astype(BF16)
    k_out = (k * jnp.exp(b_end - b)).astype(BF16)
    d_end = jnp.exp(b_end)
    v = v_ref[...].astype(BF16)
    nt = (((1,), (1,)), ((), ()))
    tn = (((0,), (0,)), ((), ()))
    for hd in range(GLA_H):
        ks = slice(hd * GLA_HK, (hd + 1) * GLA_HK)
        vs = slice(hd * GLA_HV, (hd + 1) * GLA_HV)
        scores = lax.dot_general(q_intra[:, ks], k_intra[:, ks], nt, preferred_element_type=F32)
        scores = jnp.where(causal, scores, 0.0).astype(BF16)
        state = s_ref[hd]
        o_h = _dot(scores, v[:, vs]) + lax.dot_general(q_in[:, ks], state.astype(BF16), nt,
                                                       preferred_element_type=F32)
        o_ref[:, vs] = o_h
        s_ref[hd] = state * d_end[:, ks] + lax.dot_general(v[:, vs], k_out[:, ks], tn,
                                                           preferred_element_type=F32)


def _gla_call(q, k, v, gd, g_up_pad, g_b, reverse):
    C = GLA_CHUNK
    n_lat = SEQ // C
    n_ctx = CTX_LEN // C

    def blk(i):
        if reverse:
            return n_lat + n_ctx - 1 - i
        return jnp.where(i < n_ctx, n_lat + i, i - n_ctx)

    def spec(width):
        return pl.BlockSpec((None, C, width), lambda b, i: (b, blk(i), 0))

    return pl.pallas_call(
        functools.partial(_gla_kernel, reverse=reverse),
        grid=(BATCH, n_lat + n_ctx),
        in_specs=[spec(GLA_DK), spec(GLA_DK), spec(GLA_DV), spec(GLA_GPAD),
                  _const_spec((GLA_GPAD, GLA_DK)), _const_spec((1, GLA_DK))],
        out_specs=spec(GLA_DV),
        out_shape=jax.ShapeDtypeStruct((BATCH, NTOK, GLA_DV), F32),
        scratch_shapes=[pltpu.VMEM((GLA_H, GLA_HV, GLA_HK), F32)],
        compiler_params=_cparams(2),
        name="gla_scan",
    )(q, k, v, gd, g_up_pad, g_b)


def _odd_out_kernel(h_ref, m_ref, of_ref, ob_ref, og_ref, ng_ref, w_ref, lg_ref, lb_ref, o_ref):
    o = of_ref[...] + ob_ref[...]
    og = og_ref[...]
    parts = []
    for hd in range(GLA_H):
        o_h = o[:, hd * GLA_HV:(hd + 1) * GLA_HV]
        ms = jnp.mean(o_h * o_h, axis=-1, keepdims=True)
        parts.append(o_h * lax.rsqrt(ms + GLA_EPS) * ng_ref[...])
    y = jnp.concatenate(parts, axis=-1) * _silu(og)
    mix = _dot(y.astype(BF16), w_ref[...])
    r = DN_ALPHA * h_ref[...] + m_ref[5:6, :] * mix
    o_ref[...] = _layer_norm_rows(r, lg_ref[...], lb_ref[...])


def _odd_out_call(h, mods, layer, o_f, o_b, og, norm_g, w_out, ln_g, ln_b):
    return pl.pallas_call(
        _odd_out_kernel,
        grid=(BATCH, NT),
        in_specs=[
            _tok_spec(D_MODEL), _mod_spec(layer),
            _tok_spec(GLA_DV), _tok_spec(GLA_DV), _tok_spec(GLA_DV),
            _const_spec((1, GLA_HV)), _const_spec((GLA_DV, D_MODEL)),
            _const_spec((1, D_MODEL)), _const_spec((1, D_MODEL)),
        ],
        out_specs=_tok_spec(D_MODEL),
        out_shape=jax.ShapeDtypeStruct((BATCH, NTOK, D_MODEL), F32),
        compiler_params=_cparams(2),
        name="odd_out",
    )(h, mods, o_f, o_b, og, norm_g[None, :], w_out.astype(BF16), ln_g, ln_b)


def _raster_to_columns(h):
    lat = h[:, :SEQ].reshape(BATCH, SEQ // GRID_W, GRID_W, D_MODEL).swapaxes(1, 2).reshape(BATCH, SEQ, D_MODEL)
    return jnp.concatenate([lat, h[:, SEQ:]], axis=1)


def _columns_to_raster(h):
    lat = h[:, :SEQ].reshape(BATCH, GRID_W, SEQ // GRID_W, D_MODEL).swapaxes(1, 2).reshape(BATCH, SEQ, D_MODEL)
    return jnp.concatenate([lat, h[:, SEQ:]], axis=1)


def kernel(x, c, ctx, c_ctx, ada_w, ada_b, ln_g, ln_b, ffn_wg, ffn_wu, ffn_wd, ev_w_in, ev_w_out, hy_conv_w, hy_conv_b, hy_f_w1, hy_f_b1, hy_f_freq1, hy_f_w2, hy_f_b2, hy_f_freq2, hy_f_w3, hy_skip, rw_mu, rw_w0, rw_w_up, rw_a0, rw_a_up, rw_g_up, rw_k_k, rw_k_a, rw_r_k, rw_gn_g, rw_gn_b, od_w_in, od_w_out, gla_g_up, gla_g_b, gla_norm_g):
    h = jnp.concatenate([x, ctx], axis=1)
    s = jnp.concatenate([c, c_ctx[None, :], jnp.zeros((MOD_ROWS - BATCH - 1, D_MODEL), F32)], axis=0)
    mods = _ada_call(s, ada_w, ada_b).reshape(DEPTH, MOD_ROWS, 9, D_MODEL)

    for l in range(DEPTH):
        lg = lambda i: ln_g[l, i][None, :]
        lb = lambda i: ln_b[l, i][None, :]
        h = _ffn_call(h, mods, l, 0, *_ffn_weights(ffn_wg[l, 0], ffn_wu[l, 0], ffn_wd[l, 0]), lg(0), lb(0))

        if l % 2 == 0:
            e = l // 2
            z_hy, z_rw = _inproj_call(h, mods, l, ev_w_in[e].astype(BF16), (HY_IN, RW_IN))
            hy = _hyena(z_hy, hy_conv_w[e], hy_conv_b[e], hy_f_w1[e], hy_f_b1[e], hy_f_freq1[e], hy_f_w2[e],
                        hy_f_b2[e], hy_f_freq2[e], hy_f_w3[e], hy_skip[e])
            (r, v, nkk, g, bonus, w_f, kd_f, ka_f, w_b, kd_b, ka_b) = _rw_prep_call(
                z_rw, rw_mu[e], rw_k_k[e], rw_k_a[e], rw_r_k[e], rw_w0[e], rw_a0[e], rw_w_up[e], rw_a_up[e],
                rw_g_up[e])
            rs, vs, ns = _to_scan(r), _to_scan(v), _to_scan(nkk)
            o_f = _from_scan(_rw_scan_call(rs, vs, ns, _to_scan(w_f), _to_scan(kd_f), _to_scan(ka_f), False))
            o_b = _from_scan(_rw_scan_call(rs, vs, ns, _to_scan(w_b), _to_scan(kd_b), _to_scan(ka_b), True))
            h = _even_out_call(h, mods, l, hy, o_f, o_b, bonus, g, rw_gn_g[e], rw_gn_b[e], ev_w_out[e],
                               lg(1), lb(1))
        else:
            o = l // 2
            w = od_w_in[o]
            n_qkv = 2 * GLA_DK + GLA_DV
            n_gate = 2 * GLA_GATE_LORA
            w_re = jnp.concatenate([w[:, :n_qkv], w[:, n_qkv + n_gate:], w[:, n_qkv:n_qkv + n_gate],
                                    jnp.zeros((D_MODEL, GLA_GPAD - n_gate), F32)], axis=1).astype(BF16)
            hc = _raster_to_columns(h)
            q, k, v, og, gd = _inproj_call(hc, mods, l, w_re, OD_WIDTHS)
            pad = jnp.zeros((GLA_GPAD - n_gate, GLA_DK), F32)
            zero = jnp.zeros((GLA_GATE_LORA, GLA_DK), F32)
            gup_f = jnp.concatenate([gla_g_up[o, 0], zero, pad], axis=0)
            gup_b = jnp.concatenate([zero, gla_g_up[o, 1], pad], axis=0)
            o_f = _gla_call(q, k, v, gd, gup_f, gla_g_b[o, 0][None, :], False)
            o_b = _gla_call(q, k, v, gd, gup_b, gla_g_b[o, 1][None, :], True)
            hc = _odd_out_call(hc, mods, l, o_f, o_b, og, gla_norm_g[o], od_w_out[o], lg(1), lb(1))
            h = _columns_to_raster(hc)

        h = _ffn_call(h, mods, l, 6, *_ffn_weights(ffn_wg[l, 1], ffn_wu[l, 1], ffn_wd[l, 1]), lg(2), lb(2))
    return h[:, :SEQ]
```

```python
import functools
import math

import jax
import jax.numpy as jnp
from jax import lax
from jax.experimental import pallas as pl
from jax.experimental.pallas import tpu as pltpu

F32 = jnp.float32
BF16 = jnp.bfloat16
HIGHEST = lax.Precision.HIGHEST

D_MODEL = 1024
BATCH = 16
SEQ = 2048
DEPTH = 4
GRID_W = 64
CTX_LEN = 256
DN_ALPHA = (2 * DEPTH) ** 0.25
LN_EPS = 1e-6
D_FF = 2816

HY_W = D_MODEL // 2
HY_ORDER = 2
HY_IN = (HY_ORDER + 1) * HY_W
HY_EMB = 33
HY_FO = 64
HY_TARGET = 1e-2
HY_FAST = 0.3
HY_SLOW = 1.5

RW_W = D_MODEL - HY_W
RW_N = 64
RW_H = RW_W // RW_N
RW_DECAY_LORA = 64
RW_A_LORA = 64
RW_GATE_LORA = 128
RW_GN_EPS = 64e-5
RW_IN = 3 * RW_W + 2 * RW_DECAY_LORA + 2 * RW_A_LORA + RW_GATE_LORA
EV_IN = HY_IN + RW_IN

GLA_H = 4
GLA_DK = D_MODEL // 2
GLA_DV = D_MODEL
GLA_HK = GLA_DK // GLA_H
GLA_HV = GLA_DV // GLA_H
GLA_GATE_LORA = 16
GLA_NORMALIZER = 16.0
GLA_CHUNK = 64
GLA_EPS = 1e-5

LANES = 128
SUBLANES = 8

NTOK = SEQ + CTX_LEN
TM = 256
NT = NTOK // TM
NT_LAT = SEQ // TM
MOD_ROWS = 24
TF = 256
NF = D_FF // TF
SCAN_TB = 32
BH = BATCH * RW_H
HY_S = 256
HY_CT = 256
VMEM_LIMIT = 56 * 1024 * 1024


def _cparams(n_axes):
    return pltpu.CompilerParams(dimension_semantics=("arbitrary",) * n_axes,
                                vmem_limit_bytes=VMEM_LIMIT)


def _layer_norm_rows(r, g, b):
    mu = jnp.mean(r, axis=-1, keepdims=True)
    xc = r - mu
    var = jnp.mean(xc * xc, axis=-1, keepdims=True)
    return xc * lax.rsqrt(var + LN_EPS) * g + b


def _silu(x):
    return x * jax.nn.sigmoid(x)


def _split_bf16(x):
    hi = x.astype(BF16)
    lo = (x - hi.astype(F32)).astype(BF16)
    return hi, lo


def _dot(a, b):
    return jnp.dot(a, b, preferred_element_type=F32)


def _dot_f32(a, b):
    return jnp.dot(a, b, preferred_element_type=F32, precision=HIGHEST)


def _dot3(a_hi, a_lo, b_hi, b_lo):
    return _dot(a_hi, b_hi) + (_dot(a_lo, b_hi) + _dot(a_hi, b_lo))


def _mod_spec(layer):
    def index(b, t):
        return (layer, jnp.where(t >= NT_LAT, BATCH, b), 0, 0)
    return pl.BlockSpec((None, None, 9, D_MODEL), index)


def _tok_spec(width, col=0):
    return pl.BlockSpec((None, TM, width), lambda b, t: (b, t, col))


def _const_spec(shape):
    nd = len(shape)
    return pl.BlockSpec(shape, lambda b, t: (0,) * nd)


def _ada_kernel(s_ref, w_ref, b_ref, o_ref):
    s = _silu(s_ref[...])
    o_ref[...] = _dot_f32(s, w_ref[...]) + b_ref[...]


def _ada_call(s, ada_w, ada_b):
    tn = 2304
    n_blk = 9 * D_MODEL // tn
    return pl.pallas_call(
        _ada_kernel,
        grid=(DEPTH, n_blk),
        in_specs=[
            pl.BlockSpec((MOD_ROWS, D_MODEL), lambda l, j: (0, 0)),
            pl.BlockSpec((None, D_MODEL, tn), lambda l, j: (l, 0, j)),
            pl.BlockSpec((None, 1, tn), lambda l, j: (l, 0, j)),
        ],
        out_specs=pl.BlockSpec((None, MOD_ROWS, tn), lambda l, j: (l, 0, j)),
        out_shape=jax.ShapeDtypeStruct((DEPTH, MOD_ROWS, 9 * D_MODEL), F32),
        compiler_params=_cparams(2),
        name="ada_mod",
    )(s, ada_w, ada_b.reshape(DEPTH, 1, 9 * D_MODEL))


def _ffn_kernel(h_ref, m_ref, wg_ref, wu_ref, wd_ref, g_ref, b_ref, o_ref, *, mi):
    h = h_ref[...]
    shift = m_ref[mi:mi + 1, :]
    scale = m_ref[mi + 1:mi + 2, :]
    gate = m_ref[mi + 2:mi + 3, :]
    hm = (h * (1.0 + scale) + shift).astype(BF16)
    acc = jnp.zeros((TM, D_MODEL), F32)
    for j in range(NF):
        a = _silu(_dot(hm, wg_ref[j])) * _dot(hm, wu_ref[j])
        acc = acc + _dot(a.astype(BF16), wd_ref[j])
    r = DN_ALPHA * h + (0.5 * gate) * acc
    o_ref[...] = _layer_norm_rows(r, g_ref[...], b_ref[...])


def _ffn_call(h, mods, layer, mi, wg, wu, wd, ln_g, ln_b):
    return pl.pallas_call(
        functools.partial(_ffn_kernel, mi=mi),
        grid=(BATCH, NT),
        in_specs=[
            _tok_spec(D_MODEL),
            _mod_spec(layer),
            _const_spec((NF, D_MODEL, TF)),
            _const_spec((NF, D_MODEL, TF)),
            _const_spec((NF, TF, D_MODEL)),
            _const_spec((1, D_MODEL)),
            _const_spec((1, D_MODEL)),
        ],
        out_specs=_tok_spec(D_MODEL),
        out_shape=jax.ShapeDtypeStruct((BATCH, NTOK, D_MODEL), F32),
        compiler_params=_cparams(2),
        name="ffn",
    )(h, mods, wg, wu, wd, ln_g, ln_b)


def _ffn_weights(wg, wu, wd):
    wg = wg.astype(BF16).reshape(D_MODEL, NF, TF).transpose(1, 0, 2)
    wu = wu.astype(BF16).reshape(D_MODEL, NF, TF).transpose(1, 0, 2)
    wd = wd.astype(BF16).reshape(NF, TF, D_MODEL)
    return wg, wu, wd


def _inproj_kernel(h_ref, m_ref, w_ref, *o_refs, widths):
    u = (h_ref[...] * (1.0 + m_ref[4:5, :]) + m_ref[3:4, :]).astype(BF16)
    off = 0
    for o_ref, wd in zip(o_refs, widths):
        o_ref[...] = _dot(u, w_ref[:, off:off + wd])
        off += wd


def _inproj_call(h, mods, layer, w, widths):
    n_in = sum(widths)
    return pl.pallas_call(
        functools.partial(_inproj_kernel, widths=widths),
        grid=(BATCH, NT),
        in_specs=[_tok_spec(D_MODEL), _mod_spec(layer), _const_spec((D_MODEL, n_in))],
        out_specs=[_tok_spec(wd) for wd in widths],
        out_shape=[jax.ShapeDtypeStruct((BATCH, NTOK, wd), F32) for wd in widths],
        compiler_params=_cparams(2),
        name="mixer_in",
    )(h, mods, w)


def _hyena_feature_table(L):
    t = jnp.linspace(0.0, 1.0, L, dtype=F32)[:, None]
    bands = (HY_EMB - 1) // 2
    w = 2.0 * math.pi * jnp.arange(L, dtype=F32)[:, None] / L
    f = jnp.linspace(1e-4, bands - 1, bands, dtype=F32)[None, :]
    z = jnp.concatenate([t, jnp.cos(f * w), -jnp.sin(f * w)], axis=-1)
    z = jnp.pad(z, ((0, 0), (0, LANES - HY_EMB)))
    deltas = jnp.linspace(math.log(HY_TARGET) / HY_SLOW, math.log(HY_TARGET) / HY_FAST, HY_W, dtype=F32)
    neg = (L - jnp.arange(L)) % L
    return z, t, z[neg], t[neg], jnp.abs(deltas)[None, :]


def _hy_filter_kernel(zp_ref, tp_ref, zn_ref, tn_ref, d_ref, w1_ref, b1_ref, f1_ref, w2_ref, b2_ref, f2_ref,
                      w3_ref, g_ref, *, L):
    def hidden(z):
        h = jnp.sin(f1_ref[...] * (_dot_f32(z, w1_ref[...]) + b1_ref[...]))
        return jnp.sin(f2_ref[...] * (_dot_f32(h, w2_ref[...]) + b2_ref[...]))

    h_pos = hidden(zp_ref[...])
    h_neg = hidden(zn_ref[...])
    win_pos = jnp.exp(-tp_ref[...] * d_ref[...])
    win_neg = jnp.exp(-tn_ref[...] * d_ref[...])
    rows = lax.broadcasted_iota(jnp.int32, win_neg.shape, 0)
    for o in range(HY_ORDER):
        base = o * 2 * HY_W
        g_ref[o, L:2 * L, :] = _dot_f32(h_pos, w3_ref[:, base:base + HY_W]) * win_pos
        h_bwd = _dot_f32(h_neg, w3_ref[:, base + HY_W:base + 2 * HY_W]) * win_neg
        g_ref[o, 0:L, :] = jnp.where(rows == 0, 0.0, h_bwd)


def _hy_filter_call(L, w1, b1, f1, w2, b2, f2, w3):
    zp, tp, zn, tn, deltas = _hyena_feature_table(L)
    w1p = jnp.pad(w1, ((0, LANES - HY_EMB), (0, 0)))
    return pl.pallas_call(
        functools.partial(_hy_filter_kernel, L=L),
        out_shape=jax.ShapeDtypeStruct((HY_ORDER, 2 * L, HY_W), F32),
        compiler_params=pltpu.CompilerParams(vmem_limit_bytes=VMEM_LIMIT),
        name="hyena_filter",
    )(zp, tp, zn, tn, deltas, w1p, b1[None, :], f1[None, :], w2, b2[None, :], f2[None, :], w3)


def _block_dft_tables():
    s = HY_S
    n = 2 * s
    f = jnp.arange(s, dtype=jnp.int32)[:, None]

    def trig(cols):
        j = jnp.arange(cols, dtype=jnp.int32)[None, :]
        ang = ((f * j) % n).astype(F32) * (2.0 * math.pi / n)
        alt = jnp.where(j % 2 == 0, 1.0, -1.0).astype(F32)
        return j, jnp.cos(ang), jnp.sin(ang), alt

    _, cos, sin, alt = trig(s)
    fwd = jnp.concatenate([cos, jnp.where(f == 0, alt, -sin)], axis=0)
    scale = jnp.where(f == 0, 1.0 / n, 2.0 / n).astype(F32)
    inv = jnp.concatenate([(scale * cos).T, jnp.where(f == 0, alt / n, -scale * sin).T], axis=1)
    j, cos, sin, alt = trig(n)
    sgn = jnp.where(f % 2 == 0, 1.0, -1.0).astype(F32)
    taps = jnp.concatenate([sgn * cos, jnp.where(f == 0, alt, -sgn * sin)], axis=0)
    taps = jnp.where(j == 0, 0.0, taps)
    return _split_bf16(fwd), _split_bf16(inv), _split_bf16(taps)


def _hy_spectrum_kernel(th_ref, tl_ref, lo_ref, hi_ref, k_ref):
    s = HY_S
    a_hi, a_lo = _split_bf16(lo_ref[...])
    b_hi, b_lo = _split_bf16(hi_ref[...])
    k_ref[...] = (_dot3(th_ref[:, :s], tl_ref[:, :s], a_hi, a_lo)
                  + _dot3(th_ref[:, s:], tl_ref[:, s:], b_hi, b_lo))


def _hy_spectrum_call(L, taps_tab, taps):
    s = HY_S
    nd = 2 * (L // s) - 1
    th, tl = taps_tab
    return pl.pallas_call(
        _hy_spectrum_kernel,
        grid=(HY_ORDER, nd),
        in_specs=[
            pl.BlockSpec((2 * s, 2 * s), lambda o, d: (0, 0)),
            pl.BlockSpec((2 * s, 2 * s), lambda o, d: (0, 0)),
            pl.BlockSpec((None, s, HY_W), lambda o, d: (o, d, 0)),
            pl.BlockSpec((None, s, HY_W), lambda o, d: (o, d + 1, 0)),
        ],
        out_specs=pl.BlockSpec((None, None, 2 * s, HY_W), lambda o, d: (o, d, 0, 0)),
        out_shape=jax.ShapeDtypeStruct((HY_ORDER, nd, 2 * s, HY_W), F32),
        compiler_params=_cparams(2),
        name="hyena_spectrum",
    )(th, tl, taps, taps)


def _conv3_rows(z, w, b):
    n = z.shape[0]
    rows = lax.broadcasted_iota(jnp.int32, z.shape, 0)
    prev = jnp.where(rows == 0, 0.0, pltpu.roll(z, 1, 0))
    nxt = jnp.where(rows == n - 1, 0.0, pltpu.roll(z, n - 1, 0))
    return prev * w[0:1, :] + z * w[1:2, :] + nxt * w[2:3, :] + b


def _hy_conv_kernel(u_ref, x_ref, cwu_ref, cbu_ref, cwx_ref, cbx_ref, skip_ref, k_ref, fh_ref, fl_ref,
                    ih_ref, il_ref, o_ref, u_sc, xf_ref, yf_ref, *, m, conv_u):
    s = HY_S
    ct = o_ref.shape[-1]
    if conv_u:
        u_sc[...] = _conv3_rows(u_ref[...], cwu_ref[...], cbu_ref[...])
        src = u_sc
    else:
        src = u_ref
    o_ref[...] = _conv3_rows(x_ref[...], cwx_ref[...], cbx_ref[...])

    for jb in range(m):
        u_hi, u_lo = _split_bf16(src[jb * s:(jb + 1) * s, :])
        xf_ref[jb] = _dot3(fh_ref[...], fl_ref[...], u_hi, u_lo)

    skip = skip_ref[...]
    for ib in range(m):
        def chunk(r, carry):
            re = pl.ds(pl.multiple_of(r * SUBLANES, SUBLANES), SUBLANES)
            im = pl.ds(pl.multiple_of(s + r * SUBLANES, SUBLANES), SUBLANES)
            acc_re = jnp.zeros((SUBLANES, ct), F32)
            acc_im = jnp.zeros((SUBLANES, ct), F32)
            for jb in range(m):
                d = ib - jb + m - 1
                x_re, x_im = xf_ref[jb, re, :], xf_ref[jb, im, :]
                k_re, k_im = k_ref[d, re, :], k_ref[d, im, :]
                acc_re = acc_re + (x_re * k_re - x_im * k_im)
                acc_im = acc_im + (x_re * k_im + x_im * k_re)
            yf_ref[re, :] = acc_re
            yf_ref[im, :] = acc_im
            return carry

        lax.fori_loop(0, s // SUBLANES, chunk, 0)
        dc = jnp.zeros((1, ct), F32)
        ny = jnp.zeros((1, ct), F32)
        for jb in range(m):
            d = ib - jb + m - 1
            dc = dc + xf_ref[jb, 0:1, :] * k_ref[d, 0:1, :]
            ny = ny + xf_ref[jb, s:s + 1, :] * k_ref[d, s:s + 1, :]
        yf_ref[0:1, :] = dc
        yf_ref[s:s + 1, :] = ny
        y_hi, y_lo = _split_bf16(yf_ref[...])
        conv = _dot3(ih_ref[...], il_ref[...], y_hi, y_lo)
        rows = slice(ib * s, (ib + 1) * s)
        o_ref[rows, :] = o_ref[rows, :] * (conv + skip * src[rows, :])


def _hy_conv_call(u, u_row_blk, u_part, z_hy, row_blk, x_part, L, order, conv_u, conv_w, conv_b, skip,
                  spectrum, fwd_tab, inv_tab):
    s = HY_S
    m = L // s
    nd = 2 * m - 1
    ct = HY_CT
    nct = HY_W // ct
    fh, fl = fwd_tab
    ih, il = inv_tab

    def cspec(rows, part):
        return pl.BlockSpec((rows, ct), lambda c, b: (0, part * nct + c))

    tab = lambda shape: pl.BlockSpec(shape, lambda c, b: (0, 0))
    return pl.pallas_call(
        functools.partial(_hy_conv_kernel, m=m, conv_u=conv_u),
        grid=(nct, BATCH),
        in_specs=[
            pl.BlockSpec((None, L, ct), lambda c, b: (b, u_row_blk, u_part * nct + c)),
            pl.BlockSpec((None, L, ct), lambda c, b: (b, row_blk, x_part * nct + c)),
            cspec(3, u_part), cspec(1, u_part), cspec(3, x_part), cspec(1, x_part),
            pl.BlockSpec((None, 1, ct), lambda c, b: (order, 0, c)),
            pl.BlockSpec((None, nd, 2 * s, ct), lambda c, b: (order, 0, 0, c)),
            tab((2 * s, s)), tab((2 * s, s)), tab((s, 2 * s)), tab((s, 2 * s)),
        ],
        out_specs=pl.BlockSpec((None, L, ct), lambda c, b: (b, 0, c)),
        out_shape=jax.ShapeDtypeStruct((BATCH, L, HY_W), F32),
        scratch_shapes=[
            pltpu.VMEM((L, ct), F32),
            pltpu.VMEM((m, 2 * s, ct), F32),
            pltpu.VMEM((2 * s, ct), F32),
        ],
        compiler_params=_cparams(2),
        name="hyena_conv",
    )(u, z_hy, conv_w, conv_b, conv_w, conv_b, skip[:, None, :], spectrum, fh, fl, ih, il)


def _hyena(z_hy, conv_w, conv_b, f_w1, f_b1, f_freq1, f_w2, f_b2, f_freq2, f_w3, skip):
    fwd_tab, inv_tab, taps_tab = _block_dft_tables()
    conv_b = conv_b[None, :]
    outs = []
    for L, row_blk in ((SEQ, 0), (CTX_LEN, SEQ // CTX_LEN)):
        taps = _hy_filter_call(L, f_w1, f_b1, f_freq1, f_w2, f_b2, f_freq2, f_w3)
        spectrum = _hy_spectrum_call(L, taps_tab, taps)
        y1 = _hy_conv_call(z_hy, row_blk, 0, z_hy, row_blk, 1, L, 0, True, conv_w, conv_b, skip, spectrum,
                           fwd_tab, inv_tab)
        outs.append(_hy_conv_call(y1, 0, 0, z_hy, row_blk, 2, L, 1, False, conv_w, conv_b, skip, spectrum,
                                  fwd_tab, inv_tab))
    return jnp.concatenate(outs, axis=1)


def _head_sum_matrix(width, head):
    i = jnp.arange(width)[:, None] // head
    j = jnp.arange(width)[None, :] // head
    return (i == j).astype(BF16)


def _group_sum(x, ones_ref):
    hi, lo = _split_bf16(x)
    return _dot(hi, ones_ref[...]) + _dot(lo, ones_ref[...])


def _rw_prep_kernel(z_ref, zp_ref, zn_ref, mu_ref, kk_ref, ka_ref, rk_ref, w0_ref, a0_ref, wup_ref, aup_ref,
                    gup_ref, ones_ref,
                    r_ref, v_ref, nkk_ref, g_ref, bonus_ref, wf_ref, kf_ref, af_ref, wb_ref, kb_ref, ab_ref):
    t = pl.program_id(1)
    z = z_ref[...]
    first = (t == 0) | (t == NT_LAT)
    last = (t == NT_LAT - 1) | (t == NT - 1)
    prev_row = jnp.where(first, 0.0, zp_ref[SUBLANES - 1:SUBLANES, :])
    next_row = jnp.where(last, 0.0, zn_ref[0:1, :])
    rows = lax.broadcasted_iota(jnp.int32, z.shape, 0)
    prev = jnp.where(rows == 0, prev_row, pltpu.roll(z, 1, 0))
    nxt = jnp.where(rows == TM - 1, next_row, pltpu.roll(z, TM - 1, 0))
    z = z + mu_ref[...] * (0.5 * (prev + nxt) - z)

    W = RW_W
    r = z[:, 0:W]
    k = z[:, W:2 * W]
    v = z[:, 2 * W:3 * W]
    o = 3 * W
    w_lora = _dot_f32(jnp.tanh(z[:, o:o + 2 * RW_DECAY_LORA]), wup_ref[...])
    o += 2 * RW_DECAY_LORA
    a_lora = _dot_f32(z[:, o:o + 2 * RW_A_LORA], aup_ref[...])
    o += 2 * RW_A_LORA
    g = _dot_f32(jax.nn.sigmoid(z[:, o:o + RW_GATE_LORA]), gup_ref[...])

    kk = k * kk_ref[...]
    norm = jnp.sqrt(_group_sum(kk * kk, ones_ref))
    kk = kk / jnp.maximum(norm, 1e-12)

    k_sum = jnp.zeros_like(k)
    for d, (w_ref, kd_ref, ad_ref) in enumerate(((wf_ref, kf_ref, af_ref), (wb_ref, kb_ref, ab_ref))):
        y = -(w0_ref[d:d + 1, :] + w_lora[:, d * W:(d + 1) * W])
        softplus = jnp.maximum(y, 0.0) + jnp.log(1.0 + jnp.exp(-jnp.abs(y)))
        w_log = -softplus - 0.5
        a = jax.nn.sigmoid(a0_ref[d:d + 1, :] + a_lora[:, d * W:(d + 1) * W])
        kd = k * (1.0 + (a - 1.0) * ka_ref[...])
        w_ref[...] = jnp.exp(-jnp.exp(w_log))
        kd_ref[...] = kd
        ad_ref[...] = kk * a
        k_sum = k_sum + kd

    r_ref[...] = r
    v_ref[...] = v
    nkk_ref[...] = -kk
    g_ref[...] = g
    bonus_ref[...] = _group_sum(r * k_sum * rk_ref[...], ones_ref) * v


def _rw_prep_call(z_rw, mu, k_k, k_a, r_k, w0, a0, w_up, a_up, g_up):
    W = RW_W
    zero = jnp.zeros((RW_DECAY_LORA, W), F32)
    wup = jnp.concatenate([jnp.concatenate([w_up[0], zero], 1), jnp.concatenate([zero, w_up[1]], 1)], 0)
    aup = jnp.concatenate([jnp.concatenate([a_up[0], zero], 1), jnp.concatenate([zero, a_up[1]], 1)], 0)
    blocks_per_tile = TM // SUBLANES
    n_row_blocks = NTOK // SUBLANES
    out = jax.ShapeDtypeStruct((BATCH, NTOK, W), F32)
    return pl.pallas_call(
        _rw_prep_kernel,
        grid=(BATCH, NT),
        in_specs=[
            _tok_spec(RW_IN),
            pl.BlockSpec((None, SUBLANES, RW_IN),
                         lambda b, t: (b, jnp.maximum(t * blocks_per_tile - 1, 0), 0)),
            pl.BlockSpec((None, SUBLANES, RW_IN),
                         lambda b, t: (b, jnp.minimum((t + 1) * blocks_per_tile, n_row_blocks - 1), 0)),
            _const_spec((1, RW_IN)),
            _const_spec((1, W)), _const_spec((1, W)), _const_spec((1, W)),
            _const_spec((2, W)), _const_spec((2, W)),
            _const_spec((2 * RW_DECAY_LORA, 2 * W)), _const_spec((2 * RW_A_LORA, 2 * W)),
            _const_spec((RW_GATE_LORA, W)),
            _const_spec((W, W)),
        ],
        out_specs=[_tok_spec(W)] * 11,
        out_shape=[out] * 11,
        compiler_params=_cparams(2),
        name="rwkv_prepare",
    )(z_rw, z_rw, z_rw, mu[None, :], k_k[None, :], k_a[None, :], r_k.reshape(1, W), w0, a0, wup, aup, g_up,
      _head_sum_matrix(W, RW_N))


def _rw_scan_kernel(r_ref, v_ref, nkk_ref, w_ref, kd_ref, ka_ref, o_ref, s_ref, *, reverse):
    @pl.when(pl.program_id(0) == 0)
    def _():
        s_ref[...] = jnp.zeros_like(s_ref)

    def step(i, carry):
        t = SCAN_TB - 1 - i if reverse else i
        sa = jnp.zeros((RW_N, BH), F32)
        for k in range(RW_N):
            sa = sa + s_ref[k] * nkk_ref[t, k:k + 1, :]
        vv = v_ref[t]
        out = jnp.zeros((RW_N, BH), F32)
        for k in range(RW_N):
            s_k = (s_ref[k] * w_ref[t, k:k + 1, :] + sa * ka_ref[t, k:k + 1, :]
                   + vv * kd_ref[t, k:k + 1, :])
            s_ref[k] = s_k
            out = out + s_k * r_ref[t, k:k + 1, :]
        o_ref[t] = out
        return carry

    lax.fori_loop(0, SCAN_TB, step, 0)


def _rw_scan_call(r, v, nkk, w, kd, ka, reverse):
    n_lat = SEQ // SCAN_TB
    n_ctx = CTX_LEN // SCAN_TB

    def index(i):
        if reverse:
            blk = jnp.where(i < n_ctx, n_lat + n_ctx - 1 - i, n_lat + n_ctx - 1 - i)
        else:
            blk = jnp.where(i < n_ctx, n_lat + i, i - n_ctx)
        return (blk, 0, 0)

    spec = pl.BlockSpec((SCAN_TB, RW_N, BH), index)
    return pl.pallas_call(
        functools.partial(_rw_scan_kernel, reverse=reverse),
        grid=(n_lat + n_ctx,),
        in_specs=[spec] * 6,
        out_specs=spec,
        out_shape=jax.ShapeDtypeStruct((NTOK, RW_N, BH), F32),
        scratch_shapes=[pltpu.VMEM((RW_N, RW_N, BH), F32)],
        compiler_params=_cparams(1),
        name="rwkv_scan",
    )(r, v, nkk, w, kd, ka)


def _to_scan(x):
    return x.reshape(BATCH, NTOK, RW_H, RW_N).transpose(1, 3, 0, 2).reshape(NTOK, RW_N, BH)


def _from_scan(x):
    return x.reshape(NTOK, RW_N, BATCH, RW_H).transpose(2, 0, 3, 1).reshape(BATCH, NTOK, RW_W)


def _even_out_kernel(h_ref, m_ref, hy_ref, of_ref, ob_ref, bonus_ref, g_ref, gng_ref, gnb_ref, ones_ref,
                     why_ref, wrw_ref, lg_ref, lb_ref, o_ref):
    o = of_ref[...] + ob_ref[...]
    mu = _group_sum(o, ones_ref) * (1.0 / RW_N)
    oc = o - mu
    var = _group_sum(oc * oc, ones_ref) * (1.0 / RW_N)
    y = oc * lax.rsqrt(var + RW_GN_EPS) * gng_ref[...] + gnb_ref[...]
    rw = (y + bonus_ref[...]) * g_ref[...]
    mix = _dot(hy_ref[...].astype(BF16), why_ref[...]) + _dot(rw.astype(BF16), wrw_ref[...])
    r = DN_ALPHA * h_ref[...] + m_ref[5:6, :] * mix
    o_ref[...] = _layer_norm_rows(r, lg_ref[...], lb_ref[...])


def _even_out_call(h, mods, layer, hy, o_f, o_b, bonus, g, gn_g, gn_b, w_out, ln_g, ln_b):
    W = RW_W
    w_out = w_out.astype(BF16)
    return pl.pallas_call(
        _even_out_kernel,
        grid=(BATCH, NT),
        in_specs=[
            _tok_spec(D_MODEL), _mod_spec(layer),
            _tok_spec(HY_W), _tok_spec(W), _tok_spec(W), _tok_spec(W), _tok_spec(W),
            _const_spec((1, W)), _const_spec((1, W)), _const_spec((W, W)),
            _const_spec((HY_W, D_MODEL)), _const_spec((W, D_MODEL)),
            _const_spec((1, D_MODEL)), _const_spec((1, D_MODEL)),
        ],
        out_specs=_tok_spec(D_MODEL),
        out_shape=jax.ShapeDtypeStruct((BATCH, NTOK, D_MODEL), F32),
        compiler_params=_cparams(2),
        name="even_out",
    )(h, mods, hy, o_f, o_b, bonus, g, gn_g[None, :], gn_b[None, :], _head_sum_matrix(W, RW_N),
      w_out[:HY_W], w_out[HY_W:], ln_g, ln_b)


GLA_GPAD = LANES
OD_WIDTHS = (GLA_DK, GLA_DK, GLA_DV, GLA_DV, GLA_GPAD)


def _gla_kernel(q_ref, k_ref, v_ref, gd_ref, gup_ref, gb_ref, o_ref, s_ref, *, reverse):
    C = GLA_CHUNK

    @pl.when(pl.program_id(1) == 0)
    def _():
        s_ref[...] = jnp.zeros_like(s_ref)

    logit = _dot_f32(gd_ref[...], gup_ref[...]) + gb_ref[...]
    log_sig = jnp.minimum(logit, 0.0) - jnp.log(1.0 + jnp.exp(-jnp.abs(logit)))
    g = log_sig * (1.0 / GLA_NORMALIZER)
    ri = lax.broadcasted_iota(jnp.int32, (C, C), 0)
    ci = lax.broadcasted_iota(jnp.int32, (C, C), 1)
    causal = (ci >= ri) if reverse else (ci <= ri)
    b = _dot_f32(causal.astype(F32), g)
    mid, end = (C - 1 - C // 2, 0) if reverse else (C // 2, C - 1)
    b_mid = b[mid:mid + 1, :]
    b_end = b[end:end + 1, :]
    q = q_ref[...] * (GLA_HK ** -0.5)
    k = k_ref[...]
    q_intra = (q * jnp.exp(b - b_mid)).astype(BF16)
    k_intra = (k * jnp.exp(b_mid - b)).astype(BF16)
    q_in = (q * jnp.exp(b)).astype(BF16)
    k_out = (k * jnp.exp(b_end - b)).astype(BF16)
    d_end = jnp.exp(b_end)
    v = v_ref[...].astype(BF16)
    nt = (((1,), (1,)), ((), ()))
    tn = (((0,), (0,)), ((), ()))
    for hd in range(GLA_H):
        ks = slice(hd * GLA_HK, (hd + 1) * GLA_HK)
        vs = slice(hd * GLA_HV, (hd + 1) * GLA_HV)
        scores = lax.dot_general(q_intra[:, ks], k_intra[:, ks], nt, preferred_element_type=F32)
        scores = jnp.where(causal, scores, 0.0).astype(BF16)
        state = s_ref[hd]
        o_h = _dot(scores, v[:, vs]) + lax.dot_general(q_in[:, ks], state.astype(BF16), nt,
                                                       preferred_element_type=F32)
        o_ref[:, vs] = o_h
        s_ref[hd] = state * d_end[:, ks] + lax.dot_general(v[:, vs], k_out[:, ks], tn,
                                                           preferred_element_type=F32)


def _gla_call(q, k, v, gd, g_up_pad, g_b, reverse):
    C = GLA_CHUNK
    n_lat = SEQ // C
    n_ctx = CTX_LEN // C

    def blk(i):
        if reverse:
            return n_lat + n_ctx - 1 - i
        return jnp.where(i < n_ctx, n_lat + i, i - n_ctx)

    def spec(width):
        return pl.BlockSpec((None, C, width), lambda b, i: (b, blk(i), 0))

    return pl.pallas_call(
        functools.partial(_gla_kernel, reverse=reverse),
        grid=(BATCH, n_lat + n_ctx),
        in_specs=[spec(GLA_DK), spec(GLA_DK), spec(GLA_DV), spec(GLA_GPAD),
                  _const_spec((GLA_GPAD, GLA_DK)), _const_spec((1, GLA_DK))],
        out_specs=spec(GLA_DV),
        out_shape=jax.ShapeDtypeStruct((BATCH, NTOK, GLA_DV), F32),
        scratch_shapes=[pltpu.VMEM((GLA_H, GLA_HV, GLA_HK), F32)],
        compiler_params=_cparams(2),
        name="gla_scan",
    )(q, k, v, gd, g_up_pad, g_b)


def _odd_out_kernel(h_ref, m_ref, of_ref, ob_ref, og_ref, ng_ref, w_ref, lg_ref, lb_ref, o_ref):
    o = of_ref[...] + ob_ref[...]
    og = og_ref[...]
    parts = []
    for hd in range(GLA_H):
        o_h = o[:, hd * GLA_HV:(hd + 1) * GLA_HV]
        ms = jnp.mean(o_h * o_h, axis=-1, keepdims=True)
        parts.append(o_h * lax.rsqrt(ms + GLA_EPS) * ng_ref[...])
    y = jnp.concatenate(parts, axis=-1) * _silu(og)
    mix = _dot(y.astype(BF16), w_ref[...])
    r = DN_ALPHA * h_ref[...] + m_ref[5:6, :] * mix
    o_ref[...] = _layer_norm_rows(r, lg_ref[...], lb_ref[...])


def _odd_out_call(h, mods, layer, o_f, o_b, og, norm_g, w_out, ln_g, ln_b):
    return pl.pallas_call(
        _odd_out_kernel,
        grid=(BATCH, NT),
        in_specs=[
            _tok_spec(D_MODEL), _mod_spec(layer),
            _tok_spec(GLA_DV), _tok_spec(GLA_DV), _tok_spec(GLA_DV),
            _const_spec((1, GLA_HV)), _const_spec((GLA_DV, D_MODEL)),
            _const_spec((1, D_MODEL)), _const_spec((1, D_MODEL)),
        ],
        out_specs=_tok_spec(D_MODEL),
        out_shape=jax.ShapeDtypeStruct((BATCH, NTOK, D_MODEL), F32),
        compiler_params=_cparams(2),
        name="odd_out",
    )(h, mods, o_f, o_b, og, norm_g[None, :], w_out.astype(BF16), ln_g, ln_b)


def _raster_to_columns(h):
    lat = h[:, :SEQ].reshape(BATCH, SEQ // GRID_W, GRID_W, D_MODEL).swapaxes(1, 2).reshape(BATCH, SEQ, D_MODEL)
    return jnp.concatenate([lat, h[:, SEQ:]], axis=1)


def _columns_to_raster(h):
    lat = h[:, :SEQ].reshape(BATCH, GRID_W, SEQ // GRID_W, D_MODEL).swapaxes(1, 2).reshape(BATCH, SEQ, D_MODEL)
    return jnp.concatenate([lat, h[:, SEQ:]], axis=1)


def kernel(x, c, ctx, c_ctx, ada_w, ada_b, ln_g, ln_b, ffn_wg, ffn_wu, ffn_wd, ev_w_in, ev_w_out, hy_conv_w, hy_conv_b, hy_f_w1, hy_f_b1, hy_f_freq1, hy_f_w2, hy_f_b2, hy_f_freq2, hy_f_w3, hy_skip, rw_mu, rw_w0, rw_w_up, rw_a0, rw_a_up, rw_g_up, rw_k_k, rw_k_a, rw_r_k, rw_gn_g, rw_gn_b, od_w_in, od_w_out, gla_g_up, gla_g_b, gla_norm_g):
    h = jnp.concatenate([x, ctx], axis=1)
    s = jnp.concatenate([c, c_ctx[None, :], jnp.zeros((MOD_ROWS - BATCH - 1, D_MODEL), F32)], axis=0)
    mods = _ada_call(s, ada_w, ada_b).reshape(DEPTH, MOD_ROWS, 9, D_MODEL)

    for l in range(DEPTH):
        lg = lambda i: ln_g[l, i][None, :]
        lb = lambda i: ln_b[l, i][None, :]
        h = _ffn_call(h, mods, l, 0, *_ffn_weights(ffn_wg[l, 0], ffn_wu[l, 0], ffn_wd[l, 0]), lg(0), lb(0))

        if l % 2 == 0:
            e = l // 2
            z_hy, z_rw = _inproj_call(h, mods, l, ev_w_in[e].astype(BF16), (HY_IN, RW_IN))
            hy = _hyena(z_hy, hy_conv_w[e], hy_conv_b[e], hy_f_w1[e], hy_f_b1[e], hy_f_freq1[e], hy_f_w2[e],
                        hy_f_b2[e], hy_f_freq2[e], hy_f_w3[e], hy_skip[e])
            (r, v, nkk, g, bonus, w_f, kd_f, ka_f, w_b, kd_b, ka_b) = _rw_prep_call(
                z_rw, rw_mu[e], rw_k_k[e], rw_k_a[e], rw_r_k[e], rw_w0[e], rw_a0[e], rw_w_up[e], rw_a_up[e],
                rw_g_up[e])
            rs, vs, ns = _to_scan(r), _to_scan(v), _to_scan(nkk)
            o_f = _from_scan(_rw_scan_call(rs, vs, ns, _to_scan(w_f), _to_scan(kd_f), _to_scan(ka_f), False))
            o_b = _from_scan(_rw_scan_call(rs, vs, ns, _to_scan(w_b), _to_scan(kd_b), _to_scan(ka_b), True))
            h = _even_out_call(h, mods, l, hy, o_f, o_b, bonus, g, rw_gn_g[e], rw_gn_b[e], ev_w_out[e],
                               lg(1), lb(1))
        else:
            o = l // 2
            w = od_w_in[o]
            n_qkv = 2 * GLA_DK + GLA_DV
            n_gate = 2 * GLA_GATE_LORA
            w_re = jnp.concatenate([w[:, :n_qkv], w[:, n_qkv + n_gate:], w[:, n_qkv:n_qkv + n_gate],
                                    jnp.zeros((D_MODEL, GLA_GPAD - n_gate), F32)], axis=1).astype(BF16)
            hc = _raster_to_columns(h)
            q, k, v, og, gd = _inproj_call(hc, mods, l, w_re, OD_WIDTHS)
            pad = jnp.zeros((GLA_GPAD - n_gate, GLA_DK), F32)
            zero = jnp.zeros((GLA_GATE_LORA, GLA_DK), F32)
            gup_f = jnp.concatenate([gla_g_up[o, 0], zero, pad], axis=0)
            gup_b = jnp.concatenate([zero, gla_g_up[o, 1], pad], axis=0)
            o_f = _gla_call(q, k, v, gd, gup_f, gla_g_b[o, 0][None, :], False)
            o_b = _gla_call(q, k, v, gd, gup_b, gla_g_b[o, 1][None, :], True)
            hc = _odd_out_call(hc, mods, l, o_f, o_b, og, gla_norm_g[o], od_w_out[o], lg(1), lb(1))
            h = _columns_to_raster(hc)

        h = _ffn_call(h, mods, l, 6, *_ffn_weights(ffn_wg[l, 1], ffn_wu[l, 1], ffn_wd[l, 1]), lg(2), lb(2))
    return h[:, :SEQ]
```

```python
import functools
import math
from typing import NamedTuple

import jax
import jax.numpy as jnp
from jax import lax
from jax.experimental import pallas as pl
from jax.experimental.pallas import tpu as pltpu

F32 = jnp.float32
BF16 = jnp.bfloat16
HIGHEST = lax.Precision.HIGHEST

D_MODEL = 1024
BATCH = 16
SEQ = 2048
DEPTH = 4
GRID_W = 64
CTX_LEN = 256
DN_ALPHA = (2 * DEPTH) ** 0.25
LN_EPS = 1e-6
D_FF = 2816

HY_W = D_MODEL // 2
HY_ORDER = 2
HY_IN = (HY_ORDER + 1) * HY_W
HY_EMB = 33
HY_FO = 64
HY_TARGET = 1e-2
HY_FAST = 0.3
HY_SLOW = 1.5

RW_W = D_MODEL - HY_W
RW_N = 64
RW_H = RW_W // RW_N
RW_DECAY_LORA = 64
RW_A_LORA = 64
RW_GATE_LORA = 128
RW_GN_EPS = 64e-5
RW_IN = 3 * RW_W + 2 * RW_DECAY_LORA + 2 * RW_A_LORA + RW_GATE_LORA
EV_IN = HY_IN + RW_IN

GLA_H = 4
GLA_DK = D_MODEL // 2
GLA_DV = D_MODEL
GLA_HK = GLA_DK // GLA_H
GLA_HV = GLA_DV // GLA_H
GLA_GATE_LORA = 16
GLA_NORMALIZER = 16.0
GLA_CHUNK = 64
GLA_EPS = 1e-5

LANES = 128
SUBLANES = 8

MOD_ROWS = 24
TF = 256
NF = D_FF // TF
SCAN_TB = 32
BH = BATCH * RW_H
HY_S = 256
HY_CT = 256
VMEM_LIMIT = 56 * 1024 * 1024


class _Stream(NamedTuple):
    groups: int
    rows: int
    tm: int
    ctx: bool

    @property
    def grid(self):
        return (self.groups, self.rows // self.tm)


LAT = _Stream(BATCH, SEQ, 512, False)
CTX = _Stream(BATCH // 2, 2 * CTX_LEN, 512, True)
LAT_SEG = _Stream(BATCH, SEQ, 256, False)
CTX_SEG = _Stream(BATCH, CTX_LEN, 256, True)


def _pair_ctx(a):
    return a.reshape(BATCH // 2, 2 * CTX_LEN, a.shape[-1])


def _unpair_ctx(a):
    return a.reshape(BATCH, CTX_LEN, a.shape[-1])


def _cparams(n_axes):
    return pltpu.CompilerParams(dimension_semantics=("arbitrary",) * n_axes,
                                vmem_limit_bytes=VMEM_LIMIT)


def _layer_norm_rows(r, g, b):
    mu = jnp.mean(r, axis=-1, keepdims=True)
    xc = r - mu
    var = jnp.mean(xc * xc, axis=-1, keepdims=True)
    return xc * lax.rsqrt(var + LN_EPS) * g + b


def _silu(x):
    return x * jax.nn.sigmoid(x)


def _split_bf16(x):
    hi = x.astype(BF16)
    lo = (x - hi.astype(F32)).astype(BF16)
    return hi, lo


def _dot(a, b):
    return jnp.dot(a, b, preferred_element_type=F32)


def _dot_f32(a, b):
    return jnp.dot(a, b, preferred_element_type=F32, precision=HIGHEST)


def _dot3(a_hi, a_lo, b_hi, b_lo):
    return _dot(a_hi, b_hi) + (_dot(a_lo, b_hi) + _dot(a_hi, b_lo))


def _mod_spec(layer, st):
    def index(g, t):
        return (layer, BATCH if st.ctx else g, 0, 0)
    return pl.BlockSpec((None, None, 9, D_MODEL), index)


def _tok_spec(st, width, col=0):
    return pl.BlockSpec((None, st.tm, width), lambda g, t: (g, t, col))


def _const_spec(shape):
    nd = len(shape)
    return pl.BlockSpec(shape, lambda g, t: (0,) * nd)


def _tok_shape(st, width):
    return jax.ShapeDtypeStruct((st.groups, st.rows, width), F32)


def _ada_kernel(s_ref, w_ref, b_ref, o_ref):
    s = _silu(s_ref[...])
    o_ref[...] = _dot_f32(s, w_ref[...]) + b_ref[...]


def _ada_call(s, ada_w, ada_b):
    tn = 2304
    n_blk = 9 * D_MODEL // tn
    return pl.pallas_call(
        _ada_kernel,
        grid=(DEPTH, n_blk),
        in_specs=[
            pl.BlockSpec((MOD_ROWS, D_MODEL), lambda l, j: (0, 0)),
            pl.BlockSpec((None, D_MODEL, tn), lambda l, j: (l, 0, j)),
            pl.BlockSpec((None, 1, tn), lambda l, j: (l, 0, j)),
        ],
        out_specs=pl.BlockSpec((None, MOD_ROWS, tn), lambda l, j: (l, 0, j)),
        out_shape=jax.ShapeDtypeStruct((DEPTH, MOD_ROWS, 9 * D_MODEL), F32),
        compiler_params=_cparams(2),
        name="ada_mod",
    )(s, ada_w, ada_b.reshape(DEPTH, 1, 9 * D_MODEL))


def _ffn_kernel(h_ref, m_ref, wg_ref, wu_ref, wd_ref, g_ref, b_ref, o_ref, *, mi):
    h = h_ref[...]
    shift = m_ref[mi:mi + 1, :]
    scale = m_ref[mi + 1:mi + 2, :]
    gate = m_ref[mi + 2:mi + 3, :]
    hm = (h * (1.0 + scale) + shift).astype(BF16)
    acc = jnp.zeros(h.shape, F32)
    for j in range(NF):
        a = _silu(_dot(hm, wg_ref[j])) * _dot(hm, wu_ref[j])
        acc = acc + _dot(a.astype(BF16), wd_ref[j])
    r = DN_ALPHA * h + (0.5 * gate) * acc
    o_ref[...] = _layer_norm_rows(r, g_ref[...], b_ref[...])


def _ffn_call(st, h, mods, layer, mi, weights, ln_g, ln_b):
    wg, wu, wd = weights
    return pl.pallas_call(
        functools.partial(_ffn_kernel, mi=mi),
        grid=st.grid,
        in_specs=[
            _tok_spec(st, D_MODEL),
            _mod_spec(layer, st),
            _const_spec((NF, D_MODEL, TF)),
            _const_spec((NF, D_MODEL, TF)),
            _const_spec((NF, TF, D_MODEL)),
            _const_spec((1, D_MODEL)),
            _const_spec((1, D_MODEL)),
        ],
        out_specs=_tok_spec(st, D_MODEL),
        out_shape=_tok_shape(st, D_MODEL),
        compiler_params=_cparams(2),
        name="ffn",
    )(h, mods, wg, wu, wd, ln_g, ln_b)


def _ffn_weights(wg, wu, wd):
    wg = wg.astype(BF16).reshape(D_MODEL, NF, TF).transpose(1, 0, 2)
    wu = wu.astype(BF16).reshape(D_MODEL, NF, TF).transpose(1, 0, 2)
    wd = wd.astype(BF16).reshape(NF, TF, D_MODEL)
    return wg, wu, wd


def _inproj_kernel(h_ref, m_ref, w_ref, *o_refs, widths):
    u = (h_ref[...] * (1.0 + m_ref[4:5, :]) + m_ref[3:4, :]).astype(BF16)
    off = 0
    for o_ref, wd in zip(o_refs, widths):
        o_ref[...] = _dot(u, w_ref[:, off:off + wd])
        off += wd


def _inproj_call(st, h, mods, layer, w, widths):
    n_in = sum(widths)
    return pl.pallas_call(
        functools.partial(_inproj_kernel, widths=widths),
        grid=st.grid,
        in_specs=[_tok_spec(st, D_MODEL), _mod_spec(layer, st), _const_spec((D_MODEL, n_in))],
        out_specs=[_tok_spec(st, wd) for wd in widths],
        out_shape=[_tok_shape(st, wd) for wd in widths],
        compiler_params=_cparams(2),
        name="mixer_in",
    )(h, mods, w)


def _hyena_feature_table(L):
    t = jnp.linspace(0.0, 1.0, L, dtype=F32)[:, None]
    bands = (HY_EMB - 1) // 2
    w = 2.0 * math.pi * jnp.arange(L, dtype=F32)[:, None] / L
    f = jnp.linspace(1e-4, bands - 1, bands, dtype=F32)[None, :]
    z = jnp.concatenate([t, jnp.cos(f * w), -jnp.sin(f * w)], axis=-1)
    z = jnp.pad(z, ((0, 0), (0, LANES - HY_EMB)))
    deltas = jnp.linspace(math.log(HY_TARGET) / HY_SLOW, math.log(HY_TARGET) / HY_FAST, HY_W, dtype=F32)
    neg = (L - jnp.arange(L)) % L
    return z, t, z[neg], t[neg], jnp.abs(deltas)[None, :]


def _hy_filter_kernel(zp_ref, tp_ref, zn_ref, tn_ref, d_ref, w1_ref, b1_ref, f1_ref, w2_ref, b2_ref, f2_ref,
                      w3_ref, g_ref, *, L):
    def hidden(z):
        h = jnp.sin(f1_ref[...] * (_dot_f32(z, w1_ref[...]) + b1_ref[...]))
        return jnp.sin(f2_ref[...] * (_dot_f32(h, w2_ref[...]) + b2_ref[...]))

    h_pos = hidden(zp_ref[...])
    h_neg = hidden(zn_ref[...])
    win_pos = jnp.exp(-tp_ref[...] * d_ref[...])
    win_neg = jnp.exp(-tn_ref[...] * d_ref[...])
    rows = lax.broadcasted_iota(jnp.int32, win_neg.shape, 0)
    for o in range(HY_ORDER):
        base = o * 2 * HY_W
        g_ref[o, L:2 * L, :] = _dot_f32(h_pos, w3_ref[:, base:base + HY_W]) * win_pos
        h_bwd = _dot_f32(h_neg, w3_ref[:, base + HY_W:base + 2 * HY_W]) * win_neg
        g_ref[o, 0:L, :] = jnp.where(rows == 0, 0.0, h_bwd)


def _hy_filter_call(L, w1, b1, f1, w2, b2, f2, w3):
    zp, tp, zn, tn, deltas = _hyena_feature_table(L)
    w1p = jnp.pad(w1, ((0, LANES - HY_EMB), (0, 0)))
    return pl.pallas_call(
        functools.partial(_hy_filter_kernel, L=L),
        out_shape=jax.ShapeDtypeStruct((HY_ORDER, 2 * L, HY_W), F32),
        compiler_params=pltpu.CompilerParams(vmem_limit_bytes=VMEM_LIMIT),
        name="hyena_filter",
    )(zp, tp, zn, tn, deltas, w1p, b1[None, :], f1[None, :], w2, b2[None, :], f2[None, :], w3)


def _block_dft_tables():
    s = HY_S
    n = 2 * s
    f = jnp.arange(s, dtype=jnp.int32)[:, None]

    def trig(cols):
        j = jnp.arange(cols, dtype=jnp.int32)[None, :]
        ang = ((f * j) % n).astype(F32) * (2.0 * math.pi / n)
        alt = jnp.where(j % 2 == 0, 1.0, -1.0).astype(F32)
        return j, jnp.cos(ang), jnp.sin(ang), alt

    _, cos, sin, alt = trig(s)
    fwd = jnp.concatenate([cos, jnp.where(f == 0, alt, -sin)], axis=0)
    scale = jnp.where(f == 0, 1.0 / n, 2.0 / n).astype(F32)
    inv = jnp.concatenate([(scale * cos).T, jnp.where(f == 0, alt / n, -scale * sin).T], axis=1)
    j, cos, sin, alt = trig(n)
    sgn = jnp.where(f % 2 == 0, 1.0, -1.0).astype(F32)
    taps = jnp.concatenate([sgn * cos, jnp.where(f == 0, alt, -sgn * sin)], axis=0)
    taps = jnp.where(j == 0, 0.0, taps)
    return _split_bf16(fwd), _split_bf16(inv), _split_bf16(taps)


def _hy_spectrum_kernel(th_ref, tl_ref, lo_ref, hi_ref, k_ref):
    s = HY_S
    a_hi, a_lo = _split_bf16(lo_ref[...])
    b_hi, b_lo = _split_bf16(hi_ref[...])
    k_ref[...] = (_dot3(th_ref[:, :s], tl_ref[:, :s], a_hi, a_lo)
                  + _dot3(th_ref[:, s:], tl_ref[:, s:], b_hi, b_lo))


def _hy_spectrum_call(L, taps_tab, taps):
    s = HY_S
    nd = 2 * (L // s) - 1
    th, tl = taps_tab
    return pl.pallas_call(
        _hy_spectrum_kernel,
        grid=(HY_ORDER, nd),
        in_specs=[
            pl.BlockSpec((2 * s, 2 * s), lambda o, d: (0, 0)),
            pl.BlockSpec((2 * s, 2 * s), lambda o, d: (0, 0)),
            pl.BlockSpec((None, s, HY_W), lambda o, d: (o, d, 0)),
            pl.BlockSpec((None, s, HY_W), lambda o, d: (o, d + 1, 0)),
        ],
        out_specs=pl.BlockSpec((None, None, 2 * s, HY_W), lambda o, d: (o, d, 0, 0)),
        out_shape=jax.ShapeDtypeStruct((HY_ORDER, nd, 2 * s, HY_W), F32),
        compiler_params=_cparams(2),
        name="hyena_spectrum",
    )(th, tl, taps, taps)


def _conv3_rows(z, w, b):
    n = z.shape[0]
    rows = lax.broadcasted_iota(jnp.int32, z.shape, 0)
    prev = jnp.where(rows == 0, 0.0, pltpu.roll(z, 1, 0))
    nxt = jnp.where(rows == n - 1, 0.0, pltpu.roll(z, n - 1, 0))
    return prev * w[0:1, :] + z * w[1:2, :] + nxt * w[2:3, :] + b


def _hy_conv_kernel(u_ref, x_ref, cwu_ref, cbu_ref, cwx_ref, cbx_ref, skip_ref, k_ref, fh_ref, fl_ref,
                    ih_ref, il_ref, o_ref, u_sc, xf_ref, yf_ref, *, m, conv_u):
    s = HY_S
    ct = o_ref.shape[-1]
    if conv_u:
        u_sc[...] = _conv3_rows(u_ref[...], cwu_ref[...], cbu_ref[...])
        src = u_sc
    else:
        src = u_ref
    o_ref[...] = _conv3_rows(x_ref[...], cwx_ref[...], cbx_ref[...])

    for jb in range(m):
        u_hi, u_lo = _split_bf16(src[jb * s:(jb + 1) * s, :])
        xf_ref[jb] = _dot3(fh_ref[...], fl_ref[...], u_hi, u_lo)

    skip = skip_ref[...]
    for ib in range(m):
        def chunk(r, carry):
            re = pl.ds(pl.multiple_of(r * SUBLANES, SUBLANES), SUBLANES)
            im = pl.ds(pl.multiple_of(s + r * SUBLANES, SUBLANES), SUBLANES)
            acc_re = jnp.zeros((SUBLANES, ct), F32)
            acc_im = jnp.zeros((SUBLANES, ct), F32)
            for jb in range(m):
                d = ib - jb + m - 1
                x_re, x_im = xf_ref[jb, re, :], xf_ref[jb, im, :]
                k_re, k_im = k_ref[d, re, :], k_ref[d, im, :]
                acc_re = acc_re + (x_re * k_re - x_im * k_im)
                acc_im = acc_im + (x_re * k_im + x_im * k_re)
            yf_ref[re, :] = acc_re
            yf_ref[im, :] = acc_im
            return carry

        lax.fori_loop(0, s // SUBLANES, chunk, 0)
        dc = jnp.zeros((1, ct), F32)
        ny = jnp.zeros((1, ct), F32)
        for jb in range(m):
            d = ib - jb + m - 1
            dc = dc + xf_ref[jb, 0:1, :] * k_ref[d, 0:1, :]
            ny = ny + xf_ref[jb, s:s + 1, :] * k_ref[d, s:s + 1, :]
        yf_ref[0:1, :] = dc
        yf_ref[s:s + 1, :] = ny
        y_hi, y_lo = _split_bf16(yf_ref[...])
        conv = _dot3(ih_ref[...], il_ref[...], y_hi, y_lo)
        rows = slice(ib * s, (ib + 1) * s)
        o_ref[rows, :] = o_ref[rows, :] * (conv + skip * src[rows, :])


def _hy_conv_call(u, u_part, z_hy, x_part, L, order, conv_u, conv_w, conv_b, skip, spectrum, fwd_tab, inv_tab):
    s = HY_S
    m = L // s
    nd = 2 * m - 1
    ct = HY_CT
    nct = HY_W // ct
    fh, fl = fwd_tab
    ih, il = inv_tab

    def cspec(rows, part):
        return pl.BlockSpec((rows, ct), lambda c, b: (0, part * nct + c))

    tab = lambda shape: pl.BlockSpec(shape, lambda c, b: (0, 0))
    return pl.pallas_call(
        functools.partial(_hy_conv_kernel, m=m, conv_u=conv_u),
        grid=(nct, BATCH),
        in_specs=[
            pl.BlockSpec((None, L, ct), lambda c, b: (b, 0, u_part * nct + c)),
            pl.BlockSpec((None, L, ct), lambda c, b: (b, 0, x_part * nct + c)),
            cspec(3, u_part), cspec(1, u_part), cspec(3, x_part), cspec(1, x_part),
            pl.BlockSpec((None, 1, ct), lambda c, b: (order, 0, c)),
            pl.BlockSpec((None, nd, 2 * s, ct), lambda c, b: (order, 0, 0, c)),
            tab((2 * s, s)), tab((2 * s, s)), tab((s, 2 * s)), tab((s, 2 * s)),
        ],
        out_specs=pl.BlockSpec((None, L, ct), lambda c, b: (b, 0, c)),
        out_shape=jax.ShapeDtypeStruct((BATCH, L, HY_W), F32),
        scratch_shapes=[
            pltpu.VMEM((L, ct), F32),
            pltpu.VMEM((m, 2 * s, ct), F32),
            pltpu.VMEM((2 * s, ct), F32),
        ],
        compiler_params=_cparams(2),
        name="hyena_conv",
    )(u, z_hy, conv_w, conv_b, conv_w, conv_b, skip[:, None, :], spectrum, fh, fl, ih, il)


def _hyena(z_hy, L, tables, conv_w, conv_b, f_w1, f_b1, f_freq1, f_w2, f_b2, f_freq2, f_w3, skip):
    fwd_tab, inv_tab, taps_tab = tables
    taps = _hy_filter_call(L, f_w1, f_b1, f_freq1, f_w2, f_b2, f_freq2, f_w3)
    spectrum = _hy_spectrum_call(L, taps_tab, taps)
    y1 = _hy_conv_call(z_hy, 0, z_hy, 1, L, 0, True, conv_w, conv_b, skip, spectrum, fwd_tab, inv_tab)
    return _hy_conv_call(y1, 0, z_hy, 2, L, 1, False, conv_w, conv_b, skip, spectrum, fwd_tab, inv_tab)


def _head_sum_matrix(width, head):
    i = jnp.arange(width)[:, None] // head
    j = jnp.arange(width)[None, :] // head
    return (i == j).astype(BF16)


def _group_sum(x, ones_ref):
    hi, lo = _split_bf16(x)
    return _dot(hi, ones_ref[...]) + _dot(lo, ones_ref[...])


def _rw_prep_kernel(z_ref, zp_ref, zn_ref, mu_ref, kk_ref, ka_ref, rk_ref, w0_ref, a0_ref, wup_ref, aup_ref,
                    gup_ref, ones_ref,
                    r_ref, v_ref, nkk_ref, g_ref, bonus_ref, wf_ref, kf_ref, af_ref, wb_ref, kb_ref, ab_ref):
    t = pl.program_id(1)
    z = z_ref[...]
    tm = z.shape[0]
    prev_row = jnp.where(t == 0, 0.0, zp_ref[SUBLANES - 1:SUBLANES, :])
    next_row = jnp.where(t == pl.num_programs(1) - 1, 0.0, zn_ref[0:1, :])
    rows = lax.broadcasted_iota(jnp.int32, z.shape, 0)
    prev = jnp.where(rows == 0, prev_row, pltpu.roll(z, 1, 0))
    nxt = jnp.where(rows == tm - 1, next_row, pltpu.roll(z, tm - 1, 0))
    z = z + mu_ref[...] * (0.5 * (prev + nxt) - z)

    W = RW_W
    r = z[:, 0:W]
    k = z[:, W:2 * W]
    v = z[:, 2 * W:3 * W]
    o = 3 * W
    w_lora = _dot_f32(jnp.tanh(z[:, o:o + 2 * RW_DECAY_LORA]), wup_ref[...])
    o += 2 * RW_DECAY_LORA
    a_lora = _dot_f32(z[:, o:o + 2 * RW_A_LORA], aup_ref[...])
    o += 2 * RW_A_LORA
    g = _dot_f32(jax.nn.sigmoid(z[:, o:o + RW_GATE_LORA]), gup_ref[...])

    kk = k * kk_ref[...]
    norm = jnp.sqrt(_group_sum(kk * kk, ones_ref))
    kk = kk / jnp.maximum(norm, 1e-12)

    k_sum = jnp.zeros_like(k)
    for d, (w_ref, kd_ref, ad_ref) in enumerate(((wf_ref, kf_ref, af_ref), (wb_ref, kb_ref, ab_ref))):
        y = -(w0_ref[d:d + 1, :] + w_lora[:, d * W:(d + 1) * W])
        softplus = jnp.maximum(y, 0.0) + jnp.log(1.0 + jnp.exp(-jnp.abs(y)))
        w_log = -softplus - 0.5
        a = jax.nn.sigmoid(a0_ref[d:d + 1, :] + a_lora[:, d * W:(d + 1) * W])
        kd = k * (1.0 + (a - 1.0) * ka_ref[...])
        w_ref[...] = jnp.exp(-jnp.exp(w_log))
        kd_ref[...] = kd
        ad_ref[...] = kk * a
        k_sum = k_sum + kd

    r_ref[...] = r
    v_ref[...] = v
    nkk_ref[...] = -kk
    g_ref[...] = g
    bonus_ref[...] = _group_sum(r * k_sum * rk_ref[...], ones_ref) * v


def _rw_prep_call(st, z_rw, params):
    W = RW_W
    blocks_per_tile = st.tm // SUBLANES
    n_row_blocks = st.rows // SUBLANES
    return pl.pallas_call(
        _rw_prep_kernel,
        grid=st.grid,
        in_specs=[
            _tok_spec(st, RW_IN),
            pl.BlockSpec((None, SUBLANES, RW_IN),
                         lambda g, t: (g, jnp.maximum(t * blocks_per_tile - 1, 0), 0)),
            pl.BlockSpec((None, SUBLANES, RW_IN),
                         lambda g, t: (g, jnp.minimum((t + 1) * blocks_per_tile, n_row_blocks - 1), 0)),
            _const_spec((1, RW_IN)),
            _const_spec((1, W)), _const_spec((1, W)), _const_spec((1, W)),
            _const_spec((2, W)), _const_spec((2, W)),
            _const_spec((2 * RW_DECAY_LORA, 2 * W)), _const_spec((2 * RW_A_LORA, 2 * W)),
            _const_spec((RW_GATE_LORA, W)),
            _const_spec((W, W)),
        ],
        out_specs=[_tok_spec(st, W)] * 11,
        out_shape=[_tok_shape(st, W)] * 11,
        compiler_params=_cparams(2),
        name="rwkv_prepare",
    )(z_rw, z_rw, z_rw, *params)


def _rw_prep_params(mu, k_k, k_a, r_k, w0, a0, w_up, a_up, g_up):
    W = RW_W
    zero = jnp.zeros((RW_DECAY_LORA, W), F32)
    wup = jnp.concatenate([jnp.concatenate([w_up[0], zero], 1), jnp.concatenate([zero, w_up[1]], 1)], 0)
    aup = jnp.concatenate([jnp.concatenate([a_up[0], zero], 1), jnp.concatenate([zero, a_up[1]], 1)], 0)
    return (mu[None, :], k_k[None, :], k_a[None, :], r_k.reshape(1, W), w0, a0, wup, aup, g_up,
            _head_sum_matrix(W, RW_N))


def _rw_scan_kernel(r_ref, v_ref, nkk_ref, w_ref, kd_ref, ka_ref, s0_ref, o_ref, sfin_ref, s_ref, sa_ref, *,
                    reverse):
    @pl.when(pl.program_id(0) == 0)
    def _():
        s_ref[...] = s0_ref[...]

    t_first = SCAN_TB - 1 if reverse else 0
    sa = jnp.zeros((RW_N, BH), F32)
    for k in range(RW_N):
        sa = sa + s_ref[k] * nkk_ref[t_first, k:k + 1, :]
    sa_ref[...] = sa

    def step(i, carry):
        t = SCAN_TB - 1 - i if reverse else i
        t_next = jnp.clip(t - 1 if reverse else t + 1, 0, SCAN_TB - 1)
        sa = sa_ref[...]
        vv = v_ref[t]
        out = jnp.zeros((RW_N, BH), F32)
        sa_next = jnp.zeros((RW_N, BH), F32)
        for k in range(RW_N):
            s_k = (s_ref[k] * w_ref[t, k:k + 1, :] + sa * ka_ref[t, k:k + 1, :]
                   + vv * kd_ref[t, k:k + 1, :])
            s_ref[k] = s_k
            out = out + s_k * r_ref[t, k:k + 1, :]
            sa_next = sa_next + s_k * nkk_ref[t_next, k:k + 1, :]
        o_ref[t] = out
        sa_ref[...] = sa_next
        return carry

    lax.fori_loop(0, SCAN_TB, step, 0)

    @pl.when(pl.program_id(0) == pl.num_programs(0) - 1)
    def _():
        sfin_ref[...] = s_ref[...]


def _rw_scan_call(r, v, nkk, w, kd, ka, s0, reverse):
    n_blk = r.shape[0] // SCAN_TB
    spec = pl.BlockSpec((SCAN_TB, RW_N, BH), lambda i: ((n_blk - 1 - i) if reverse else i, 0, 0))
    state = pl.BlockSpec((RW_N, RW_N, BH), lambda i: (0, 0, 0))
    return pl.pallas_call(
        functools.partial(_rw_scan_kernel, reverse=reverse),
        grid=(n_blk,),
        in_specs=[spec] * 6 + [state],
        out_specs=[spec, state],
        out_shape=[jax.ShapeDtypeStruct(r.shape, F32), jax.ShapeDtypeStruct((RW_N, RW_N, BH), F32)],
        scratch_shapes=[pltpu.VMEM((RW_N, RW_N, BH), F32), pltpu.VMEM((RW_N, BH), F32)],
        compiler_params=_cparams(1),
        name="rwkv_scan",
    )(r, v, nkk, w, kd, ka, s0)


def _to_scan(x):
    n = x.shape[1]
    return x.reshape(BATCH, n, RW_H, RW_N).transpose(1, 3, 0, 2).reshape(n, RW_N, BH)


def _from_scan(x):
    n = x.shape[0]
    return x.reshape(n, RW_N, BATCH, RW_H).transpose(2, 0, 3, 1).reshape(BATCH, n, RW_W)


def _rwkv_bidir(prep_ctx, prep_lat):
    outs = {"ctx": [], "lat": []}
    shared = {name: [_to_scan(p[i]) for i in range(3)] for name, p in (("ctx", prep_ctx), ("lat", prep_lat))}
    for d, reverse in enumerate((False, True)):
        state = jnp.zeros((RW_N, RW_N, BH), F32)
        for name, p in (("ctx", prep_ctx), ("lat", prep_lat)):
            dirs = [_to_scan(p[5 + 3 * d + i]) for i in range(3)]
            o, state = _rw_scan_call(*shared[name], *dirs, state, reverse)
            outs[name].append(_from_scan(o))
    return outs["ctx"], outs["lat"]


def _even_out_kernel(h_ref, m_ref, hy_ref, of_ref, ob_ref, bonus_ref, g_ref, gng_ref, gnb_ref, ones_ref,
                     why_ref, wrw_ref, lg_ref, lb_ref, o_ref):
    o = of_ref[...] + ob_ref[...]
    mu = _group_sum(o, ones_ref) * (1.0 / RW_N)
    oc = o - mu
    var = _group_sum(oc * oc, ones_ref) * (1.0 / RW_N)
    y = oc * lax.rsqrt(var + RW_GN_EPS) * gng_ref[...] + gnb_ref[...]
    rw = (y + bonus_ref[...]) * g_ref[...]
    mix = _dot(hy_ref[...].astype(BF16), why_ref[...]) + _dot(rw.astype(BF16), wrw_ref[...])
    r = DN_ALPHA * h_ref[...] + m_ref[5:6, :] * mix
    o_ref[...] = _layer_norm_rows(r, lg_ref[...], lb_ref[...])


def _even_out_call(st, h, mods, layer, hy, o_f, o_b, bonus, g, gn_g, gn_b, w_out, ln_g, ln_b):
    W = RW_W
    return pl.pallas_call(
        _even_out_kernel,
        grid=st.grid,
        in_specs=[
            _tok_spec(st, D_MODEL), _mod_spec(layer, st),
            _tok_spec(st, HY_W), _tok_spec(st, W), _tok_spec(st, W), _tok_spec(st, W), _tok_spec(st, W),
            _const_spec((1, W)), _const_spec((1, W)), _const_spec((W, W)),
            _const_spec((HY_W, D_MODEL)), _const_spec((W, D_MODEL)),
            _const_spec((1, D_MODEL)), _const_spec((1, D_MODEL)),
        ],
        out_specs=_tok_spec(st, D_MODEL),
        out_shape=_tok_shape(st, D_MODEL),
        compiler_params=_cparams(2),
        name="even_out",
    )(h, mods, hy, o_f, o_b, bonus, g, gn_g[None, :], gn_b[None, :], _head_sum_matrix(W, RW_N),
      w_out[:HY_W], w_out[HY_W:], ln_g, ln_b)


GLA_GPAD = LANES
OD_WIDTHS = (GLA_DK, GLA_DK, GLA_DV, GLA_DV, GLA_GPAD)


def _gla_kernel(q_ref, k_ref, v_ref, gd_ref, gup_ref, gb_ref, s0_ref, o_ref, sfin_ref, s_ref, *, reverse):
    C = GLA_CHUNK

    @pl.when(pl.program_id(1) == 0)
    def _():
        s_ref[...] = s0_ref[...]

    logit = _dot_f32(gd_ref[...], gup_ref[...]) + gb_ref[...]
    log_sig = jnp.minimum(logit, 0.0) - jnp.log(1.0 + jnp.exp(-jnp.abs(logit)))
    g = log_sig * (1.0 / GLA_NORMALIZER)
    ri = lax.broadcasted_iota(jnp.int32, (C, C), 0)
    ci = lax.broadcasted_iota(jnp.int32, (C, C), 1)
    causal = (ci >= ri) if reverse else (ci <= ri)
    b = _dot_f32(causal.astype(F32), g)
    mid, end = (C - 1 - C // 2, 0) if reverse else (C // 2, C - 1)
    b_mid = b[mid:mid + 1, :]
    b_end = b[end:end + 1, :]
    q = q_ref[...] * (GLA_HK ** -0.5)
    k = k_ref[...]
    q_intra = (q * jnp.exp(b - b_mid)).astype(BF16)
    k_intra = (k * jnp.exp(b_mid - b)).astype(BF16)
    q_in = (q * jnp.exp(b)).astype(BF16)
    k_out = (k * jnp.exp(b_end - b)).astype(BF16)
    d_end = jnp.exp(b_end)
    v = v_ref[...].astype(BF16)
    nt = (((1,), (1,)), ((), ()))
    tn = (((0,), (0,)), ((), ()))
    for hd in range(GLA_H):
        ks = slice(hd * GLA_HK, (hd + 1) * GLA_HK)
        vs = slice(hd * GLA_HV, (hd + 1) * GLA_HV)
        scores = lax.dot_general(q_intra[:, ks], k_intra[:, ks], nt, preferred_element_type=F32)
        scores = jnp.where(causal, scores, 0.0).astype(BF16)
        state = s_ref[hd]
        o_h = _dot(scores, v[:, vs]) + lax.dot_general(q_in[:, ks], state.astype(BF16), nt,
                                                       preferred_element_type=F32)
        o_ref[:, vs] = o_h
        s_ref[hd] = state * d_end[:, ks] + lax.dot_general(v[:, vs], k_out[:, ks], tn,
                                                           preferred_element_type=F32)

    @pl.when(pl.program_id(1) == pl.num_programs(1) - 1)
    def _():
        sfin_ref[...] = s_ref[...]


def _gla_call(q, k, v, gd, g_up_pad, g_b, s0, reverse):
    C = GLA_CHUNK
    n_chunk = q.shape[1] // C

    def spec(width):
        return pl.BlockSpec((None, C, width), lambda b, i: (b, (n_chunk - 1 - i) if reverse else i, 0))

    state = pl.BlockSpec((None, GLA_H, GLA_HV, GLA_HK), lambda b, i: (b, 0, 0, 0))
    return pl.pallas_call(
        functools.partial(_gla_kernel, reverse=reverse),
        grid=(BATCH, n_chunk),
        in_specs=[spec(GLA_DK), spec(GLA_DK), spec(GLA_DV), spec(GLA_GPAD),
                  _const_spec((GLA_GPAD, GLA_DK)), _const_spec((1, GLA_DK)), state],
        out_specs=[spec(GLA_DV), state],
        out_shape=[jax.ShapeDtypeStruct((BATCH, q.shape[1], GLA_DV), F32),
                   jax.ShapeDtypeStruct((BATCH, GLA_H, GLA_HV, GLA_HK), F32)],
        scratch_shapes=[pltpu.VMEM((GLA_H, GLA_HV, GLA_HK), F32)],
        compiler_params=_cparams(2),
        name="gla_scan",
    )(q, k, v, gd, g_up_pad, g_b, s0)


def _odd_out_kernel(h_ref, m_ref, of_ref, ob_ref, og_ref, ng_ref, w_ref, lg_ref, lb_ref, o_ref):
    o = of_ref[...] + ob_ref[...]
    og = og_ref[...]
    parts = []
    for hd in range(GLA_H):
        o_h = o[:, hd * GLA_HV:(hd + 1) * GLA_HV]
        ms = jnp.mean(o_h * o_h, axis=-1, keepdims=True)
        parts.append(o_h * lax.rsqrt(ms + GLA_EPS) * ng_ref[...])
    y = jnp.concatenate(parts, axis=-1) * _silu(og)
    mix = _dot(y.astype(BF16), w_ref[...])
    r = DN_ALPHA * h_ref[...] + m_ref[5:6, :] * mix
    o_ref[...] = _layer_norm_rows(r, lg_ref[...], lb_ref[...])


def _odd_out_call(st, h, mods, layer, o_f, o_b, og, norm_g, w_out, ln_g, ln_b):
    return pl.pallas_call(
        _odd_out_kernel,
        grid=st.grid,
        in_specs=[
            _tok_spec(st, D_MODEL), _mod_spec(layer, st),
            _tok_spec(st, GLA_DV), _tok_spec(st, GLA_DV), _tok_spec(st, GLA_DV),
            _const_spec((1, GLA_HV)), _const_spec((GLA_DV, D_MODEL)),
            _const_spec((1, D_MODEL)), _const_spec((1, D_MODEL)),
        ],
        out_specs=_tok_spec(st, D_MODEL),
        out_shape=_tok_shape(st, D_MODEL),
        compiler_params=_cparams(2),
        name="odd_out",
    )(h, mods, o_f, o_b, og, norm_g[None, :], w_out, ln_g, ln_b)


def _raster_to_columns(a):
    return a.reshape(BATCH, SEQ // GRID_W, GRID_W, a.shape[-1]).swapaxes(1, 2).reshape(a.shape)


def _columns_to_raster(a):
    return a.reshape(BATCH, GRID_W, SEQ // GRID_W, a.shape[-1]).swapaxes(1, 2).reshape(a.shape)


def kernel(x, c, ctx, c_ctx, ada_w, ada_b, ln_g, ln_b, ffn_wg, ffn_wu, ffn_wd, ev_w_in, ev_w_out, hy_conv_w, hy_conv_b, hy_f_w1, hy_f_b1, hy_f_freq1, hy_f_w2, hy_f_b2, hy_f_freq2, hy_f_w3, hy_skip, rw_mu, rw_w0, rw_w_up, rw_a0, rw_a_up, rw_g_up, rw_k_k, rw_k_a, rw_r_k, rw_gn_g, rw_gn_b, od_w_in, od_w_out, gla_g_up, gla_g_b, gla_norm_g):
    h_lat, h_ctx = x, _pair_ctx(ctx)
    s = jnp.concatenate([c, c_ctx[None, :], jnp.zeros((MOD_ROWS - BATCH - 1, D_MODEL), F32)], axis=0)
    mods = _ada_call(s, ada_w, ada_b).reshape(DEPTH, MOD_ROWS, 9, D_MODEL)
    dft_tables = _block_dft_tables()

    for l in range(DEPTH):
        last = l == DEPTH - 1
        lg = lambda i: ln_g[l, i][None, :]
        lb = lambda i: ln_b[l, i][None, :]
        w_ffn = _ffn_weights(ffn_wg[l, 0], ffn_wu[l, 0], ffn_wd[l, 0])
        h_lat = _ffn_call(LAT, h_lat, mods, l, 0, w_ffn, lg(0), lb(0))
        h_ctx = _ffn_call(CTX, h_ctx, mods, l, 0, w_ffn, lg(0), lb(0))

        if l % 2 == 0:
            e = l // 2
            w_in = ev_w_in[e].astype(BF16)
            w_out = ev_w_out[e].astype(BF16)
            hy_params = (hy_conv_w[e], hy_conv_b[e][None, :], hy_f_w1[e], hy_f_b1[e], hy_f_freq1[e], hy_f_w2[e],
                         hy_f_b2[e], hy_f_freq2[e], hy_f_w3[e], hy_skip[e])
            rw_params = _rw_prep_params(rw_mu[e], rw_k_k[e], rw_k_a[e], rw_r_k[e], rw_w0[e], rw_a0[e],
                                        rw_w_up[e], rw_a_up[e], rw_g_up[e])
            zl_hy, zl_rw = _inproj_call(LAT, h_lat, mods, l, w_in, (HY_IN, RW_IN))
            zc_hy, zc_rw = _inproj_call(CTX, h_ctx, mods, l, w_in, (HY_IN, RW_IN))
            hy_lat = _hyena(zl_hy, SEQ, dft_tables, *hy_params)
            hy_ctx = _hyena(_unpair_ctx(zc_hy), CTX_LEN, dft_tables, *hy_params)
            prep_lat = _rw_prep_call(LAT_SEG, zl_rw, rw_params)
            prep_ctx = _rw_prep_call(CTX_SEG, _unpair_ctx(zc_rw), rw_params)
            (oc_f, oc_b), (ol_f, ol_b) = _rwkv_bidir(prep_ctx, prep_lat)
            h_lat = _even_out_call(LAT, h_lat, mods, l, hy_lat, ol_f, ol_b, prep_lat[4], prep_lat[3],
                                   rw_gn_g[e], rw_gn_b[e], w_out, lg(1), lb(1))
            if not last:
                h_ctx = _even_out_call(CTX, h_ctx, mods, l, _pair_ctx(hy_ctx), _pair_ctx(oc_f), _pair_ctx(oc_b),
                                       _pair_ctx(prep_ctx[4]), _pair_ctx(prep_ctx[3]),
                                       rw_gn_g[e], rw_gn_b[e], w_out, lg(1), lb(1))
        else:
            o = l // 2
            w = od_w_in[o]
            n_qkv = 2 * GLA_DK + GLA_DV
            n_gate = 2 * GLA_GATE_LORA
            w_in = jnp.concatenate([w[:, :n_qkv], w[:, n_qkv + n_gate:], w[:, n_qkv:n_qkv + n_gate],
                                    jnp.zeros((D_MODEL, GLA_GPAD - n_gate), F32)], axis=1).astype(BF16)
            w_out = od_w_out[o].astype(BF16)
            pad = jnp.zeros((GLA_GPAD - n_gate, GLA_DK), F32)
            zero = jnp.zeros((GLA_GATE_LORA, GLA_DK), F32)
            gup = (jnp.concatenate([gla_g_up[o, 0], zero, pad], axis=0),
                   jnp.concatenate([zero, gla_g_up[o, 1], pad], axis=0))
            hc_lat = _raster_to_columns(h_lat)
            ql, kl, vl, ogl, gdl = _inproj_call(LAT, hc_lat, mods, l, w_in, OD_WIDTHS)
            qc, kc, vc, ogc, gdc = [_unpair_ctx(a) for a in _inproj_call(CTX, h_ctx, mods, l, w_in, OD_WIDTHS)]
            o_ctx, o_lat = [], []
            for d, reverse in enumerate((False, True)):
                g_b = gla_g_b[o, d][None, :]
                state = jnp.zeros((BATCH, GLA_H, GLA_HV, GLA_HK), F32)
                oc, state = _gla_call(qc, kc, vc, gdc, gup[d], g_b, state, reverse)
                ol, _ = _gla_call(ql, kl, vl, gdl, gup[d], g_b, state, reverse)
                o_ctx.append(oc)
                o_lat.append(ol)
            hc_lat = _odd_out_call(LAT, hc_lat, mods, l, o_lat[0], o_lat[1], ogl, gla_norm_g[o], w_out,
                                   lg(1), lb(1))
            h_lat = _columns_to_raster(hc_lat)
            if not last:
                h_ctx = _odd_out_call(CTX, h_ctx, mods, l, _pair_ctx(o_ctx[0]), _pair_ctx(o_ctx[1]),
                                      _pair_ctx(ogc), gla_norm_g[o], w_out, lg(1), lb(1))

        w_ffn = _ffn_weights(ffn_wg[l, 1], ffn_wu[l, 1], ffn_wd[l, 1])
        h_lat = _ffn_call(LAT, h_lat, mods, l, 6, w_ffn, lg(2), lb(2))
        if not last:
            h_ctx = _ffn_call(CTX, h_ctx, mods, l, 6, w_ffn, lg(2), lb(2))
    return h_lat
```

```python
import functools
import math
from typing import NamedTuple

import jax
import jax.numpy as jnp
from jax import lax
from jax.experimental import pallas as pl
from jax.experimental.pallas import tpu as pltpu

F32 = jnp.float32
BF16 = jnp.bfloat16
HIGHEST = lax.Precision.HIGHEST

D_MODEL = 1024
BATCH = 16
SEQ = 2048
DEPTH = 4
GRID_W = 64
CTX_LEN = 256
DN_ALPHA = (2 * DEPTH) ** 0.25
LN_EPS = 1e-6
D_FF = 2816

HY_W = D_MODEL // 2
HY_ORDER = 2
HY_IN = (HY_ORDER + 1) * HY_W
HY_EMB = 33
HY_FO = 64
HY_TARGET = 1e-2
HY_FAST = 0.3
HY_SLOW = 1.5

RW_W = D_MODEL - HY_W
RW_N = 64
RW_H = RW_W // RW_N
RW_DECAY_LORA = 64
RW_A_LORA = 64
RW_GATE_LORA = 128
RW_GN_EPS = 64e-5
RW_IN = 3 * RW_W + 2 * RW_DECAY_LORA + 2 * RW_A_LORA + RW_GATE_LORA
EV_IN = HY_IN + RW_IN

GLA_H = 4
GLA_DK = D_MODEL // 2
GLA_DV = D_MODEL
GLA_HK = GLA_DK // GLA_H
GLA_HV = GLA_DV // GLA_H
GLA_GATE_LORA = 16
GLA_NORMALIZER = 16.0
GLA_CHUNK = 64
GLA_EPS = 1e-5

LANES = 128
SUBLANES = 8

MOD_ROWS = 24
TF = 256
NF = D_FF // TF
SCAN_TB = 32
BH = BATCH * RW_H
HY_S = 256
HY_CT = 256
HY_MAC_ROWS = 16
VMEM_LIMIT = 56 * 1024 * 1024


class _Stream(NamedTuple):
    groups: int
    rows: int
    tm: int
    ctx: bool

    @property
    def grid(self):
        return (self.groups, self.rows // self.tm)


LAT = _Stream(BATCH, SEQ, 512, False)
CTX = _Stream(BATCH // 2, 2 * CTX_LEN, 512, True)
LAT_SEG = _Stream(BATCH, SEQ, 256, False)
CTX_SEG = _Stream(BATCH, CTX_LEN, 256, True)


def _pair_ctx(a):
    return a.reshape(BATCH // 2, 2 * CTX_LEN, a.shape[-1])


def _unpair_ctx(a):
    return a.reshape(BATCH, CTX_LEN, a.shape[-1])


def _cparams(n_axes):
    return pltpu.CompilerParams(dimension_semantics=("arbitrary",) * n_axes,
                                vmem_limit_bytes=VMEM_LIMIT)


def _layer_norm_rows(r, g, b):
    mu = jnp.mean(r, axis=-1, keepdims=True)
    xc = r - mu
    var = jnp.mean(xc * xc, axis=-1, keepdims=True)
    return xc * lax.rsqrt(var + LN_EPS) * g + b


def _silu(x):
    return x * jax.nn.sigmoid(x)


def _split_bf16(x):
    hi = x.astype(BF16)
    lo = (x - hi.astype(F32)).astype(BF16)
    return hi, lo


def _dot(a, b):
    return jnp.dot(a, b, preferred_element_type=F32)


def _dot_f32(a, b):
    return jnp.dot(a, b, preferred_element_type=F32, precision=HIGHEST)


def _dot3(a_hi, a_lo, b_hi, b_lo):
    return _dot(a_hi, b_hi) + (_dot(a_lo, b_hi) + _dot(a_hi, b_lo))


def _mod_spec(layer, st):
    def index(g, t):
        return (layer, BATCH if st.ctx else g, 0, 0)
    return pl.BlockSpec((None, None, 9, D_MODEL), index)


def _tok_spec(st, width, col=0):
    return pl.BlockSpec((None, st.tm, width), lambda g, t: (g, t, col))


def _const_spec(shape):
    nd = len(shape)
    return pl.BlockSpec(shape, lambda g, t: (0,) * nd)


def _tok_shape(st, width):
    return jax.ShapeDtypeStruct((st.groups, st.rows, width), F32)


def _ada_kernel(s_ref, w_ref, b_ref, o_ref):
    s = _silu(s_ref[...])
    o_ref[...] = _dot_f32(s, w_ref[...]) + b_ref[...]


def _ada_call(s, ada_w, ada_b):
    tn = 2304
    n_blk = 9 * D_MODEL // tn
    return pl.pallas_call(
        _ada_kernel,
        grid=(DEPTH, n_blk),
        in_specs=[
            pl.BlockSpec((MOD_ROWS, D_MODEL), lambda l, j: (0, 0)),
            pl.BlockSpec((None, D_MODEL, tn), lambda l, j: (l, 0, j)),
            pl.BlockSpec((None, 1, tn), lambda l, j: (l, 0, j)),
        ],
        out_specs=pl.BlockSpec((None, MOD_ROWS, tn), lambda l, j: (l, 0, j)),
        out_shape=jax.ShapeDtypeStruct((DEPTH, MOD_ROWS, 9 * D_MODEL), F32),
        compiler_params=_cparams(2),
        name="ada_mod",
    )(s, ada_w, ada_b.reshape(DEPTH, 1, 9 * D_MODEL))


def _ffn_kernel(h_ref, m_ref, wg_ref, wu_ref, wd_ref, g_ref, b_ref, o_ref, *, mi):
    h = h_ref[...]
    shift = m_ref[mi:mi + 1, :]
    scale = m_ref[mi + 1:mi + 2, :]
    gate = m_ref[mi + 2:mi + 3, :]
    hm = (h * (1.0 + scale) + shift).astype(BF16)
    acc = jnp.zeros(h.shape, F32)
    for j in range(NF):
        a = _silu(_dot(hm, wg_ref[j])) * _dot(hm, wu_ref[j])
        acc = acc + _dot(a.astype(BF16), wd_ref[j])
    r = DN_ALPHA * h + (0.5 * gate) * acc
    o_ref[...] = _layer_norm_rows(r, g_ref[...], b_ref[...])


def _ffn_call(st, h, mods, layer, mi, weights, ln_g, ln_b):
    wg, wu, wd = weights
    return pl.pallas_call(
        functools.partial(_ffn_kernel, mi=mi),
        grid=st.grid,
        in_specs=[
            _tok_spec(st, D_MODEL),
            _mod_spec(layer, st),
            _const_spec((NF, D_MODEL, TF)),
            _const_spec((NF, D_MODEL, TF)),
            _const_spec((NF, TF, D_MODEL)),
            _const_spec((1, D_MODEL)),
            _const_spec((1, D_MODEL)),
        ],
        out_specs=_tok_spec(st, D_MODEL),
        out_shape=_tok_shape(st, D_MODEL),
        compiler_params=_cparams(2),
        name="ffn",
    )(h, mods, wg, wu, wd, ln_g, ln_b)


def _ffn_weights(wg, wu, wd):
    wg = wg.astype(BF16).reshape(D_MODEL, NF, TF).transpose(1, 0, 2)
    wu = wu.astype(BF16).reshape(D_MODEL, NF, TF).transpose(1, 0, 2)
    wd = wd.astype(BF16).reshape(NF, TF, D_MODEL)
    return wg, wu, wd


def _inproj_kernel(h_ref, m_ref, w_ref, *o_refs, widths):
    u = (h_ref[...] * (1.0 + m_ref[4:5, :]) + m_ref[3:4, :]).astype(BF16)
    off = 0
    for o_ref, wd in zip(o_refs, widths):
        o_ref[...] = _dot(u, w_ref[:, off:off + wd])
        off += wd


def _inproj_call(st, h, mods, layer, w, widths):
    n_in = sum(widths)
    return pl.pallas_call(
        functools.partial(_inproj_kernel, widths=widths),
        grid=st.grid,
        in_specs=[_tok_spec(st, D_MODEL), _mod_spec(layer, st), _const_spec((D_MODEL, n_in))],
        out_specs=[_tok_spec(st, wd) for wd in widths],
        out_shape=[_tok_shape(st, wd) for wd in widths],
        compiler_params=_cparams(2),
        name="mixer_in",
    )(h, mods, w)


def _hyena_feature_table(L):
    t = jnp.linspace(0.0, 1.0, L, dtype=F32)[:, None]
    bands = (HY_EMB - 1) // 2
    w = 2.0 * math.pi * jnp.arange(L, dtype=F32)[:, None] / L
    f = jnp.linspace(1e-4, bands - 1, bands, dtype=F32)[None, :]
    z = jnp.concatenate([t, jnp.cos(f * w), -jnp.sin(f * w)], axis=-1)
    z = jnp.pad(z, ((0, 0), (0, LANES - HY_EMB)))
    deltas = jnp.linspace(math.log(HY_TARGET) / HY_SLOW, math.log(HY_TARGET) / HY_FAST, HY_W, dtype=F32)
    neg = (L - jnp.arange(L)) % L
    return z, t, z[neg], t[neg], jnp.abs(deltas)[None, :]


def _hy_filter_kernel(zp_ref, tp_ref, zn_ref, tn_ref, d_ref, w1_ref, b1_ref, f1_ref, w2_ref, b2_ref, f2_ref,
                      w3_ref, g_ref, *, L):
    def hidden(z):
        h = jnp.sin(f1_ref[...] * (_dot_f32(z, w1_ref[...]) + b1_ref[...]))
        return jnp.sin(f2_ref[...] * (_dot_f32(h, w2_ref[...]) + b2_ref[...]))

    h_pos = hidden(zp_ref[...])
    h_neg = hidden(zn_ref[...])
    win_pos = jnp.exp(-tp_ref[...] * d_ref[...])
    win_neg = jnp.exp(-tn_ref[...] * d_ref[...])
    rows = lax.broadcasted_iota(jnp.int32, win_neg.shape, 0)
    for o in range(HY_ORDER):
        base = o * 2 * HY_W
        g_ref[o, L:2 * L, :] = _dot_f32(h_pos, w3_ref[:, base:base + HY_W]) * win_pos
        h_bwd = _dot_f32(h_neg, w3_ref[:, base + HY_W:base + 2 * HY_W]) * win_neg
        g_ref[o, 0:L, :] = jnp.where(rows == 0, 0.0, h_bwd)


def _hy_filter_call(L, w1, b1, f1, w2, b2, f2, w3):
    zp, tp, zn, tn, deltas = _hyena_feature_table(L)
    w1p = jnp.pad(w1, ((0, LANES - HY_EMB), (0, 0)))
    return pl.pallas_call(
        functools.partial(_hy_filter_kernel, L=L),
        out_shape=jax.ShapeDtypeStruct((HY_ORDER, 2 * L, HY_W), F32),
        compiler_params=pltpu.CompilerParams(vmem_limit_bytes=VMEM_LIMIT),
        name="hyena_filter",
    )(zp, tp, zn, tn, deltas, w1p, b1[None, :], f1[None, :], w2, b2[None, :], f2[None, :], w3)


def _block_dft_tables():
    s = HY_S
    n = 2 * s
    f = jnp.arange(s, dtype=jnp.int32)[:, None]

    def trig(cols):
        j = jnp.arange(cols, dtype=jnp.int32)[None, :]
        ang = ((f * j) % n).astype(F32) * (2.0 * math.pi / n)
        alt = jnp.where(j % 2 == 0, 1.0, -1.0).astype(F32)
        return j, jnp.cos(ang), jnp.sin(ang), alt

    _, cos, sin, alt = trig(s)
    fwd = jnp.concatenate([cos, jnp.where(f == 0, alt, -sin)], axis=0)
    scale = jnp.where(f == 0, 1.0 / n, 2.0 / n).astype(F32)
    inv = jnp.concatenate([(scale * cos).T, jnp.where(f == 0, alt / n, -scale * sin).T], axis=1)
    j, cos, sin, alt = trig(n)
    sgn = jnp.where(f % 2 == 0, 1.0, -1.0).astype(F32)
    taps = jnp.concatenate([sgn * cos, jnp.where(f == 0, alt, -sgn * sin)], axis=0)
    taps = jnp.where(j == 0, 0.0, taps)
    return _split_bf16(fwd), _split_bf16(inv), _split_bf16(taps)


def _hy_spectrum_kernel(th_ref, tl_ref, lo_ref, hi_ref, k_ref):
    s = HY_S
    a_hi, a_lo = _split_bf16(lo_ref[...])
    b_hi, b_lo = _split_bf16(hi_ref[...])
    k_ref[...] = (_dot3(th_ref[:, :s], tl_ref[:, :s], a_hi, a_lo)
                  + _dot3(th_ref[:, s:], tl_ref[:, s:], b_hi, b_lo))


def _hy_spectrum_call(L, taps_tab, taps):
    s = HY_S
    nd = 2 * (L // s) - 1
    th, tl = taps_tab
    return pl.pallas_call(
        _hy_spectrum_kernel,
        grid=(HY_ORDER, nd),
        in_specs=[
            pl.BlockSpec((2 * s, 2 * s), lambda o, d: (0, 0)),
            pl.BlockSpec((2 * s, 2 * s), lambda o, d: (0, 0)),
            pl.BlockSpec((None, s, HY_W), lambda o, d: (o, d, 0)),
            pl.BlockSpec((None, s, HY_W), lambda o, d: (o, d + 1, 0)),
        ],
        out_specs=pl.BlockSpec((None, None, 2 * s, HY_W), lambda o, d: (o, d, 0, 0)),
        out_shape=jax.ShapeDtypeStruct((HY_ORDER, nd, 2 * s, HY_W), F32),
        compiler_params=_cparams(2),
        name="hyena_spectrum",
    )(th, tl, taps, taps)


def _conv3_rows(z, w, b):
    n = z.shape[0]
    rows = lax.broadcasted_iota(jnp.int32, z.shape, 0)
    prev = jnp.where(rows == 0, 0.0, pltpu.roll(z, 1, 0))
    nxt = jnp.where(rows == n - 1, 0.0, pltpu.roll(z, n - 1, 0))
    return prev * w[0:1, :] + z * w[1:2, :] + nxt * w[2:3, :] + b


def _hy_conv_kernel(u_ref, x_ref, cwu_ref, cbu_ref, cwx_ref, cbx_ref, skip_ref, k_ref, fh_ref, fl_ref,
                    ih_ref, il_ref, o_ref, u_sc, xf_ref, yf_ref, *, m, conv_u):
    s = HY_S
    ct = o_ref.shape[-1]
    if conv_u:
        u_sc[...] = _conv3_rows(u_ref[...], cwu_ref[...], cbu_ref[...])
        src = u_sc
    else:
        src = u_ref
    o_ref[...] = _conv3_rows(x_ref[...], cwx_ref[...], cbx_ref[...])

    for jb in range(m):
        u_hi, u_lo = _split_bf16(src[jb * s:(jb + 1) * s, :])
        xf_ref[jb] = _dot3(fh_ref[...], fl_ref[...], u_hi, u_lo)

    for ib in range(m):
        def chunk(r, carry):
            re = pl.ds(pl.multiple_of(r * HY_MAC_ROWS, HY_MAC_ROWS), HY_MAC_ROWS)
            im = pl.ds(pl.multiple_of(s + r * HY_MAC_ROWS, HY_MAC_ROWS), HY_MAC_ROWS)
            acc_re = jnp.zeros((HY_MAC_ROWS, ct), F32)
            acc_im = jnp.zeros((HY_MAC_ROWS, ct), F32)
            for jb in range(m):
                d = ib - jb + m - 1
                x_re, x_im = xf_ref[jb, re, :], xf_ref[jb, im, :]
                k_re, k_im = k_ref[d, re, :], k_ref[d, im, :]
                acc_re = acc_re + (x_re * k_re - x_im * k_im)
                acc_im = acc_im + (x_re * k_im + x_im * k_re)
            yf_ref[ib, re, :] = acc_re
            yf_ref[ib, im, :] = acc_im
            return carry

        lax.fori_loop(0, s // HY_MAC_ROWS, chunk, 0)
        dc = jnp.zeros((1, ct), F32)
        ny = jnp.zeros((1, ct), F32)
        for jb in range(m):
            d = ib - jb + m - 1
            dc = dc + xf_ref[jb, 0:1, :] * k_ref[d, 0:1, :]
            ny = ny + xf_ref[jb, s:s + 1, :] * k_ref[d, s:s + 1, :]
        yf_ref[ib, 0:1, :] = dc
        yf_ref[ib, s:s + 1, :] = ny

    skip = skip_ref[...]
    for ib in range(m):
        y_hi, y_lo = _split_bf16(yf_ref[ib])
        conv = _dot3(ih_ref[...], il_ref[...], y_hi, y_lo)
        rows = slice(ib * s, (ib + 1) * s)
        o_ref[rows, :] = o_ref[rows, :] * (conv + skip * src[rows, :])


def _hy_conv_call(u, u_part, z_hy, x_part, L, order, conv_u, conv_w, conv_b, skip, spectrum, fwd_tab, inv_tab):
    s = HY_S
    m = L // s
    nd = 2 * m - 1
    ct = HY_CT
    nct = HY_W // ct
    fh, fl = fwd_tab
    ih, il = inv_tab

    def cspec(rows, part):
        return pl.BlockSpec((rows, ct), lambda c, b: (0, part * nct + c))

    tab = lambda shape: pl.BlockSpec(shape, lambda c, b: (0, 0))
    return pl.pallas_call(
        functools.partial(_hy_conv_kernel, m=m, conv_u=conv_u),
        grid=(nct, BATCH),
        in_specs=[
            pl.BlockSpec((None, L, ct), lambda c, b: (b, 0, u_part * nct + c)),
            pl.BlockSpec((None, L, ct), lambda c, b: (b, 0, x_part * nct + c)),
            cspec(3, u_part), cspec(1, u_part), cspec(3, x_part), cspec(1, x_part),
            pl.BlockSpec((None, 1, ct), lambda c, b: (order, 0, c)),
            pl.BlockSpec((None, nd, 2 * s, ct), lambda c, b: (order, 0, 0, c)),
            tab((2 * s, s)), tab((2 * s, s)), tab((s, 2 * s)), tab((s, 2 * s)),
        ],
        out_specs=pl.BlockSpec((None, L, ct), lambda c, b: (b, 0, c)),
        out_shape=jax.ShapeDtypeStruct((BATCH, L, HY_W), F32),
        scratch_shapes=[
            pltpu.VMEM((L, ct), F32),
            pltpu.VMEM((m, 2 * s, ct), F32),
            pltpu.VMEM((m, 2 * s, ct), F32),
        ],
        compiler_params=_cparams(2),
        name="hyena_conv",
    )(u, z_hy, conv_w, conv_b, conv_w, conv_b, skip[:, None, :], spectrum, fh, fl, ih, il)


def _hyena(z_hy, L, tables, conv_w, conv_b, f_w1, f_b1, f_freq1, f_w2, f_b2, f_freq2, f_w3, skip):
    fwd_tab, inv_tab, taps_tab = tables
    taps = _hy_filter_call(L, f_w1, f_b1, f_freq1, f_w2, f_b2, f_freq2, f_w3)
    spectrum = _hy_spectrum_call(L, taps_tab, taps)
    y1 = _hy_conv_call(z_hy, 0, z_hy, 1, L, 0, True, conv_w, conv_b, skip, spectrum, fwd_tab, inv_tab)
    return _hy_conv_call(y1, 0, z_hy, 2, L, 1, False, conv_w, conv_b, skip, spectrum, fwd_tab, inv_tab)


def _head_sum_matrix(width, head):
    i = jnp.arange(width)[:, None] // head
    j = jnp.arange(width)[None, :] // head
    return (i == j).astype(BF16)


def _group_sum(x, ones_ref):
    hi, lo = _split_bf16(x)
    return _dot(hi, ones_ref[...]) + _dot(lo, ones_ref[...])


def _rw_prep_kernel(z_ref, zp_ref, zn_ref, mu_ref, kk_ref, ka_ref, rk_ref, w0_ref, a0_ref, wup_ref, aup_ref,
                    gup_ref, ones_ref,
                    r_ref, v_ref, nkk_ref, g_ref, bonus_ref, wf_ref, kf_ref, af_ref, wb_ref, kb_ref, ab_ref):
    t = pl.program_id(1)
    z = z_ref[...]
    tm = z.shape[0]
    prev_row = jnp.where(t == 0, 0.0, zp_ref[SUBLANES - 1:SUBLANES, :])
    next_row = jnp.where(t == pl.num_programs(1) - 1, 0.0, zn_ref[0:1, :])
    rows = lax.broadcasted_iota(jnp.int32, z.shape, 0)
    prev = jnp.where(rows == 0, prev_row, pltpu.roll(z, 1, 0))
    nxt = jnp.where(rows == tm - 1, next_row, pltpu.roll(z, tm - 1, 0))
    z = z + mu_ref[...] * (0.5 * (prev + nxt) - z)

    W = RW_W
    r = z[:, 0:W]
    k = z[:, W:2 * W]
    v = z[:, 2 * W:3 * W]
    o = 3 * W
    w_lora = _dot_f32(jnp.tanh(z[:, o:o + 2 * RW_DECAY_LORA]), wup_ref[...])
    o += 2 * RW_DECAY_LORA
    a_lora = _dot_f32(z[:, o:o + 2 * RW_A_LORA], aup_ref[...])
    o += 2 * RW_A_LORA
    g = _dot_f32(jax.nn.sigmoid(z[:, o:o + RW_GATE_LORA]), gup_ref[...])

    kk = k * kk_ref[...]
    norm = jnp.sqrt(_group_sum(kk * kk, ones_ref))
    kk = kk / jnp.maximum(norm, 1e-12)

    k_sum = jnp.zeros_like(k)
    for d, (w_ref, kd_ref, ad_ref) in enumerate(((wf_ref, kf_ref, af_ref), (wb_ref, kb_ref, ab_ref))):
        y = -(w0_ref[d:d + 1, :] + w_lora[:, d * W:(d + 1) * W])
        softplus = jnp.maximum(y, 0.0) + jnp.log(1.0 + jnp.exp(-jnp.abs(y)))
        w_log = -softplus - 0.5
        a = jax.nn.sigmoid(a0_ref[d:d + 1, :] + a_lora[:, d * W:(d + 1) * W])
        kd = k * (1.0 + (a - 1.0) * ka_ref[...])
        w_ref[...] = jnp.exp(-jnp.exp(w_log))
        kd_ref[...] = kd
        ad_ref[...] = kk * a
        k_sum = k_sum + kd

    r_ref[...] = r
    v_ref[...] = v
    nkk_ref[...] = -kk
    g_ref[...] = g
    bonus_ref[...] = _group_sum(r * k_sum * rk_ref[...], ones_ref) * v


def _rw_prep_call(st, z_rw, params):
    W = RW_W
    blocks_per_tile = st.tm // SUBLANES
    n_row_blocks = st.rows // SUBLANES
    return pl.pallas_call(
        _rw_prep_kernel,
        grid=st.grid,
        in_specs=[
            _tok_spec(st, RW_IN),
            pl.BlockSpec((None, SUBLANES, RW_IN),
                         lambda g, t: (g, jnp.maximum(t * blocks_per_tile - 1, 0), 0)),
            pl.BlockSpec((None, SUBLANES, RW_IN),
                         lambda g, t: (g, jnp.minimum((t + 1) * blocks_per_tile, n_row_blocks - 1), 0)),
            _const_spec((1, RW_IN)),
            _const_spec((1, W)), _const_spec((1, W)), _const_spec((1, W)),
            _const_spec((2, W)), _const_spec((2, W)),
            _const_spec((2 * RW_DECAY_LORA, 2 * W)), _const_spec((2 * RW_A_LORA, 2 * W)),
            _const_spec((RW_GATE_LORA, W)),
            _const_spec((W, W)),
        ],
        out_specs=[_tok_spec(st, W)] * 11,
        out_shape=[_tok_shape(st, W)] * 11,
        compiler_params=_cparams(2),
        name="rwkv_prepare",
    )(z_rw, z_rw, z_rw, *params)


def _rw_prep_params(mu, k_k, k_a, r_k, w0, a0, w_up, a_up, g_up):
    W = RW_W
    zero = jnp.zeros((RW_DECAY_LORA, W), F32)
    wup = jnp.concatenate([jnp.concatenate([w_up[0], zero], 1), jnp.concatenate([zero, w_up[1]], 1)], 0)
    aup = jnp.concatenate([jnp.concatenate([a_up[0], zero], 1), jnp.concatenate([zero, a_up[1]], 1)], 0)
    return (mu[None, :], k_k[None, :], k_a[None, :], r_k.reshape(1, W), w0, a0, wup, aup, g_up,
            _head_sum_matrix(W, RW_N))


def _rw_scan_kernel(r_ref, v_ref, nkk_ref, w_ref, kd_ref, ka_ref, s0_ref, o_ref, sfin_ref, s_ref, sa_ref, *,
                    reverse):
    @pl.when(pl.program_id(0) == 0)
    def _():
        s_ref[...] = s0_ref[...]

    t_first = SCAN_TB - 1 if reverse else 0
    sa = jnp.zeros((RW_N, BH), F32)
    for k in range(RW_N):
        sa = sa + s_ref[k] * nkk_ref[t_first, k:k + 1, :]
    sa_ref[...] = sa

    def step(i, carry):
        t = SCAN_TB - 1 - i if reverse else i
        t_next = jnp.clip(t - 1 if reverse else t + 1, 0, SCAN_TB - 1)
        sa = sa_ref[...]
        vv = v_ref[t]
        out = jnp.zeros((RW_N, BH), F32)
        sa_next = jnp.zeros((RW_N, BH), F32)
        for k in range(RW_N):
            s_k = (s_ref[k] * w_ref[t, k:k + 1, :] + sa * ka_ref[t, k:k + 1, :]
                   + vv * kd_ref[t, k:k + 1, :])
            s_ref[k] = s_k
            out = out + s_k * r_ref[t, k:k + 1, :]
            sa_next = sa_next + s_k * nkk_ref[t_next, k:k + 1, :]
        o_ref[t] = out
        sa_ref[...] = sa_next
        return carry

    lax.fori_loop(0, SCAN_TB, step, 0)

    @pl.when(pl.program_id(0) == pl.num_programs(0) - 1)
    def _():
        sfin_ref[...] = s_ref[...]


def _rw_scan_call(r, v, nkk, w, kd, ka, s0, reverse):
    n_blk = r.shape[0] // SCAN_TB
    spec = pl.BlockSpec((SCAN_TB, RW_N, BH), lambda i: ((n_blk - 1 - i) if reverse else i, 0, 0))
    state = pl.BlockSpec((RW_N, RW_N, BH), lambda i: (0, 0, 0))
    return pl.pallas_call(
        functools.partial(_rw_scan_kernel, reverse=reverse),
        grid=(n_blk,),
        in_specs=[spec] * 6 + [state],
        out_specs=[spec, state],
        out_shape=[jax.ShapeDtypeStruct(r.shape, F32), jax.ShapeDtypeStruct((RW_N, RW_N, BH), F32)],
        scratch_shapes=[pltpu.VMEM((RW_N, RW_N, BH), F32), pltpu.VMEM((RW_N, BH), F32)],
        compiler_params=_cparams(1),
        name="rwkv_scan",
    )(r, v, nkk, w, kd, ka, s0)


def _to_scan(x):
    n = x.shape[1]
    return x.reshape(BATCH, n, RW_H, RW_N).transpose(1, 3, 0, 2).reshape(n, RW_N, BH)


def _from_scan(x):
    n = x.shape[0]
    return x.reshape(n, RW_N, BATCH, RW_H).transpose(2, 0, 3, 1).reshape(BATCH, n, RW_W)


def _rwkv_bidir(prep_ctx, prep_lat):
    outs = {"ctx": [], "lat": []}
    shared = {name: [_to_scan(p[i]) for i in range(3)] for name, p in (("ctx", prep_ctx), ("lat", prep_lat))}
    for d, reverse in enumerate((False, True)):
        state = jnp.zeros((RW_N, RW_N, BH), F32)
        for name, p in (("ctx", prep_ctx), ("lat", prep_lat)):
            dirs = [_to_scan(p[5 + 3 * d + i]) for i in range(3)]
            o, state = _rw_scan_call(*shared[name], *dirs, state, reverse)
            outs[name].append(_from_scan(o))
    return outs["ctx"], outs["lat"]


def _even_out_kernel(h_ref, m_ref, hy_ref, of_ref, ob_ref, bonus_ref, g_ref, gng_ref, gnb_ref, ones_ref,
                     why_ref, wrw_ref, lg_ref, lb_ref, o_ref):
    o = of_ref[...] + ob_ref[...]
    mu = _group_sum(o, ones_ref) * (1.0 / RW_N)
    oc = o - mu
    var = _group_sum(oc * oc, ones_ref) * (1.0 / RW_N)
    y = oc * lax.rsqrt(var + RW_GN_EPS) * gng_ref[...] + gnb_ref[...]
    rw = (y + bonus_ref[...]) * g_ref[...]
    mix = _dot(hy_ref[...].astype(BF16), why_ref[...]) + _dot(rw.astype(BF16), wrw_ref[...])
    r = DN_ALPHA * h_ref[...] + m_ref[5:6, :] * mix
    o_ref[...] = _layer_norm_rows(r, lg_ref[...], lb_ref[...])


def _even_out_call(st, h, mods, layer, hy, o_f, o_b, bonus, g, gn_g, gn_b, w_out, ln_g, ln_b):
    W = RW_W
    return pl.pallas_call(
        _even_out_kernel,
        grid=st.grid,
        in_specs=[
            _tok_spec(st, D_MODEL), _mod_spec(layer, st),
            _tok_spec(st, HY_W), _tok_spec(st, W), _tok_spec(st, W), _tok_spec(st, W), _tok_spec(st, W),
            _const_spec((1, W)), _const_spec((1, W)), _const_spec((W, W)),
            _const_spec((HY_W, D_MODEL)), _const_spec((W, D_MODEL)),
            _const_spec((1, D_MODEL)), _const_spec((1, D_MODEL)),
        ],
        out_specs=_tok_spec(st, D_MODEL),
        out_shape=_tok_shape(st, D_MODEL),
        compiler_params=_cparams(2),
        name="even_out",
    )(h, mods, hy, o_f, o_b, bonus, g, gn_g[None, :], gn_b[None, :], _head_sum_matrix(W, RW_N),
      w_out[:HY_W], w_out[HY_W:], ln_g, ln_b)


GLA_GPAD = LANES
OD_WIDTHS = (GLA_DK, GLA_DK, GLA_DV, GLA_DV, GLA_GPAD)


def _odd_inproj_kernel(h_ref, m_ref, w_ref, gup_ref, gb_ref, q_ref, k_ref, v_ref, og_ref, gf_ref, gb_out_ref):
    u = (h_ref[...] * (1.0 + m_ref[4:5, :]) + m_ref[3:4, :]).astype(BF16)
    off = 0
    for o_ref, wd in zip((q_ref, k_ref, v_ref, og_ref), OD_WIDTHS[:4]):
        o_ref[...] = _dot(u, w_ref[:, off:off + wd])
        off += wd
    gd = _dot(u, w_ref[:, off:off + GLA_GPAD])
    logit = _dot_f32(gd, gup_ref[...]) + gb_ref[...]
    log_sig = jnp.minimum(logit, 0.0) - jnp.log(1.0 + jnp.exp(-jnp.abs(logit)))
    g = log_sig * (1.0 / GLA_NORMALIZER)
    gf_ref[...] = g[:, :GLA_DK]
    gb_out_ref[...] = g[:, GLA_DK:]


def _odd_inproj_call(st, h, mods, layer, w, gup, g_b):
    widths = OD_WIDTHS[:4] + (GLA_DK, GLA_DK)
    return pl.pallas_call(
        _odd_inproj_kernel,
        grid=st.grid,
        in_specs=[_tok_spec(st, D_MODEL), _mod_spec(layer, st), _const_spec((D_MODEL, sum(OD_WIDTHS))),
                  _const_spec((GLA_GPAD, 2 * GLA_DK)), _const_spec((1, 2 * GLA_DK))],
        out_specs=[_tok_spec(st, wd) for wd in widths],
        out_shape=[_tok_shape(st, wd) for wd in widths],
        compiler_params=_cparams(2),
        name="mixer_in_gla",
    )(h, mods, w, gup, g_b)


def _gla_chunk(q_ref, k_ref, v_ref, g_ref, o_ref, s_ref, reverse):
    C = GLA_CHUNK
    g = g_ref[...]
    ri = lax.broadcasted_iota(jnp.int32, (C, C), 0)
    ci = lax.broadcasted_iota(jnp.int32, (C, C), 1)
    causal = (ci >= ri) if reverse else (ci <= ri)
    b = _dot_f32(causal.astype(F32), g)
    mid, end = (C - 1 - C // 2, 0) if reverse else (C // 2, C - 1)
    b_mid = b[mid:mid + 1, :]
    b_end = b[end:end + 1, :]
    q = q_ref[...] * (GLA_HK ** -0.5)
    k = k_ref[...]
    q_intra = (q * jnp.exp(b - b_mid)).astype(BF16)
    k_intra = (k * jnp.exp(b_mid - b)).astype(BF16)
    q_in = (q * jnp.exp(b)).astype(BF16)
    k_out = (k * jnp.exp(b_end - b)).astype(BF16)
    d_end = jnp.exp(b_end)
    v = v_ref[...].astype(BF16)
    nt = (((1,), (1,)), ((), ()))
    tn = (((0,), (0,)), ((), ()))
    for hd in range(GLA_H):
        ks = slice(hd * GLA_HK, (hd + 1) * GLA_HK)
        vs = slice(hd * GLA_HV, (hd + 1) * GLA_HV)
        scores = lax.dot_general(q_intra[:, ks], k_intra[:, ks], nt, preferred_element_type=F32)
        scores = jnp.where(causal, scores, 0.0).astype(BF16)
        state = s_ref[hd]
        o_h = _dot(scores, v[:, vs]) + lax.dot_general(q_in[:, ks], state.astype(BF16), nt,
                                                       preferred_element_type=F32)
        o_ref[:, vs] = o_h
        s_ref[hd] = state * d_end[:, ks] + lax.dot_general(v[:, vs], k_out[:, ks], tn,
                                                           preferred_element_type=F32)


def _gla_kernel(qf_ref, kf_ref, vf_ref, gf_ref, qb_ref, kb_ref, vb_ref, gb_ref, s0f_ref, s0b_ref,
                of_ref, ob_ref, sff_ref, sfb_ref, sf_sc, sb_sc):
    @pl.when(pl.program_id(1) == 0)
    def _():
        sf_sc[...] = s0f_ref[...]
        sb_sc[...] = s0b_ref[...]

    _gla_chunk(qf_ref, kf_ref, vf_ref, gf_ref, of_ref, sf_sc, False)
    _gla_chunk(qb_ref, kb_ref, vb_ref, gb_ref, ob_ref, sb_sc, True)

    @pl.when(pl.program_id(1) == pl.num_programs(1) - 1)
    def _():
        sff_ref[...] = sf_sc[...]
        sfb_ref[...] = sb_sc[...]


def _gla_call(q, k, v, g_f, g_b, s0_f, s0_b):
    C = GLA_CHUNK
    n_chunk = q.shape[1] // C

    def spec(width, reverse):
        return pl.BlockSpec((None, C, width), lambda b, i: (b, (n_chunk - 1 - i) if reverse else i, 0))

    state = pl.BlockSpec((None, GLA_H, GLA_HV, GLA_HK), lambda b, i: (b, 0, 0, 0))
    ins = lambda reverse: [spec(GLA_DK, reverse), spec(GLA_DK, reverse), spec(GLA_DV, reverse),
                           spec(GLA_DK, reverse)]
    o_shape = jax.ShapeDtypeStruct((BATCH, q.shape[1], GLA_DV), F32)
    s_shape = jax.ShapeDtypeStruct((BATCH, GLA_H, GLA_HV, GLA_HK), F32)
    return pl.pallas_call(
        _gla_kernel,
        grid=(BATCH, n_chunk),
        in_specs=ins(False) + ins(True) + [state, state],
        out_specs=[spec(GLA_DV, False), spec(GLA_DV, True), state, state],
        out_shape=[o_shape, o_shape, s_shape, s_shape],
        scratch_shapes=[pltpu.VMEM((GLA_H, GLA_HV, GLA_HK), F32)] * 2,
        compiler_params=_cparams(2),
        name="gla_scan",
    )(q, k, v, g_f, q, k, v, g_b, s0_f, s0_b)


def _odd_out_kernel(h_ref, m_ref, of_ref, ob_ref, og_ref, ng_ref, w_ref, lg_ref, lb_ref, o_ref):
    o = of_ref[...] + ob_ref[...]
    og = og_ref[...]
    parts = []
    for hd in range(GLA_H):
        o_h = o[:, hd * GLA_HV:(hd + 1) * GLA_HV]
        ms = jnp.mean(o_h * o_h, axis=-1, keepdims=True)
        parts.append(o_h * lax.rsqrt(ms + GLA_EPS) * ng_ref[...])
    y = jnp.concatenate(parts, axis=-1) * _silu(og)
    mix = _dot(y.astype(BF16), w_ref[...])
    r = DN_ALPHA * h_ref[...] + m_ref[5:6, :] * mix
    o_ref[...] = _layer_norm_rows(r, lg_ref[...], lb_ref[...])


def _odd_out_call(st, h, mods, layer, o_f, o_b, og, norm_g, w_out, ln_g, ln_b):
    return pl.pallas_call(
        _odd_out_kernel,
        grid=st.grid,
        in_specs=[
            _tok_spec(st, D_MODEL), _mod_spec(layer, st),
            _tok_spec(st, GLA_DV), _tok_spec(st, GLA_DV), _tok_spec(st, GLA_DV),
            _const_spec((1, GLA_HV)), _const_spec((GLA_DV, D_MODEL)),
            _const_spec((1, D_MODEL)), _const_spec((1, D_MODEL)),
        ],
        out_specs=_tok_spec(st, D_MODEL),
        out_shape=_tok_shape(st, D_MODEL),
        compiler_params=_cparams(2),
        name="odd_out",
    )(h, mods, o_f, o_b, og, norm_g[None, :], w_out, ln_g, ln_b)


def _raster_to_columns(a):
    return a.reshape(BATCH, SEQ // GRID_W, GRID_W, a.shape[-1]).swapaxes(1, 2).reshape(a.shape)


def _columns_to_raster(a):
    return a.reshape(BATCH, GRID_W, SEQ // GRID_W, a.shape[-1]).swapaxes(1, 2).reshape(a.shape)


def kernel(x, c, ctx, c_ctx, ada_w, ada_b, ln_g, ln_b, ffn_wg, ffn_wu, ffn_wd, ev_w_in, ev_w_out, hy_conv_w, hy_conv_b, hy_f_w1, hy_f_b1, hy_f_freq1, hy_f_w2, hy_f_b2, hy_f_freq2, hy_f_w3, hy_skip, rw_mu, rw_w0, rw_w_up, rw_a0, rw_a_up, rw_g_up, rw_k_k, rw_k_a, rw_r_k, rw_gn_g, rw_gn_b, od_w_in, od_w_out, gla_g_up, gla_g_b, gla_norm_g):
    h_lat, h_ctx = x, _pair_ctx(ctx)
    s = jnp.concatenate([c, c_ctx[None, :], jnp.zeros((MOD_ROWS - BATCH - 1, D_MODEL), F32)], axis=0)
    mods = _ada_call(s, ada_w, ada_b).reshape(DEPTH, MOD_ROWS, 9, D_MODEL)
    dft_tables = _block_dft_tables()

    for l in range(DEPTH):
        last = l == DEPTH - 1
        lg = lambda i: ln_g[l, i][None, :]
        lb = lambda i: ln_b[l, i][None, :]
        w_ffn = _ffn_weights(ffn_wg[l, 0], ffn_wu[l, 0], ffn_wd[l, 0])
        h_lat = _ffn_call(LAT, h_lat, mods, l, 0, w_ffn, lg(0), lb(0))
        h_ctx = _ffn_call(CTX, h_ctx, mods, l, 0, w_ffn, lg(0), lb(0))

        if l % 2 == 0:
            e = l // 2
            w_in = ev_w_in[e].astype(BF16)
            w_out = ev_w_out[e].astype(BF16)
            hy_params = (hy_conv_w[e], hy_conv_b[e][None, :], hy_f_w1[e], hy_f_b1[e], hy_f_freq1[e], hy_f_w2[e],
                         hy_f_b2[e], hy_f_freq2[e], hy_f_w3[e], hy_skip[e])
            rw_params = _rw_prep_params(rw_mu[e], rw_k_k[e], rw_k_a[e], rw_r_k[e], rw_w0[e], rw_a0[e],
                                        rw_w_up[e], rw_a_up[e], rw_g_up[e])
            zl_hy, zl_rw = _inproj_call(LAT, h_lat, mods, l, w_in, (HY_IN, RW_IN))
            zc_hy, zc_rw = _inproj_call(CTX, h_ctx, mods, l, w_in, (HY_IN, RW_IN))
            hy_lat = _hyena(zl_hy, SEQ, dft_tables, *hy_params)
            hy_ctx = _hyena(_unpair_ctx(zc_hy), CTX_LEN, dft_tables, *hy_params)
            prep_lat = _rw_prep_call(LAT_SEG, zl_rw, rw_params)
            prep_ctx = _rw_prep_call(CTX_SEG, _unpair_ctx(zc_rw), rw_params)
            (oc_f, oc_b), (ol_f, ol_b) = _rwkv_bidir(prep_ctx, prep_lat)
            h_lat = _even_out_call(LAT, h_lat, mods, l, hy_lat, ol_f, ol_b, prep_lat[4], prep_lat[3],
                                   rw_gn_g[e], rw_gn_b[e], w_out, lg(1), lb(1))
            if not last:
                h_ctx = _even_out_call(CTX, h_ctx, mods, l, _pair_ctx(hy_ctx), _pair_ctx(oc_f), _pair_ctx(oc_b),
                                       _pair_ctx(prep_ctx[4]), _pair_ctx(prep_ctx[3]),
                                       rw_gn_g[e], rw_gn_b[e], w_out, lg(1), lb(1))
        else:
            o = l // 2
            w = od_w_in[o]
            n_qkv = 2 * GLA_DK + GLA_DV
            n_gate = 2 * GLA_GATE_LORA
            w_in = jnp.concatenate([w[:, :n_qkv], w[:, n_qkv + n_gate:], w[:, n_qkv:n_qkv + n_gate],
                                    jnp.zeros((D_MODEL, GLA_GPAD - n_gate), F32)], axis=1).astype(BF16)
            w_out = od_w_out[o].astype(BF16)
            pad = jnp.zeros((GLA_GPAD - n_gate, GLA_DK), F32)
            zero = jnp.zeros((GLA_GATE_LORA, GLA_DK), F32)
            gup = jnp.concatenate([jnp.concatenate([gla_g_up[o, 0], zero, pad], axis=0),
                                   jnp.concatenate([zero, gla_g_up[o, 1], pad], axis=0)], axis=1)
            g_bias = gla_g_b[o].reshape(1, 2 * GLA_DK)
            hc_lat = _raster_to_columns(h_lat)
            ql, kl, vl, ogl, gfl, gbl = _odd_inproj_call(LAT, hc_lat, mods, l, w_in, gup, g_bias)
            qc, kc, vc, ogc, gfc, gbc = [_unpair_ctx(a)
                                         for a in _odd_inproj_call(CTX, h_ctx, mods, l, w_in, gup, g_bias)]
            zero_state = jnp.zeros((BATCH, GLA_H, GLA_HV, GLA_HK), F32)
            oc_f, oc_b, s_f, s_b = _gla_call(qc, kc, vc, gfc, gbc, zero_state, zero_state)
            ol_f, ol_b, _, _ = _gla_call(ql, kl, vl, gfl, gbl, s_f, s_b)
            hc_lat = _odd_out_call(LAT, hc_lat, mods, l, ol_f, ol_b, ogl, gla_norm_g[o], w_out, lg(1), lb(1))
            h_lat = _columns_to_raster(hc_lat)
            if not last:
                h_ctx = _odd_out_call(CTX, h_ctx, mods, l, _pair_ctx(oc_f), _pair_ctx(oc_b),
                                      _pair_ctx(ogc), gla_norm_g[o], w_out, lg(1), lb(1))

        w_ffn = _ffn_weights(ffn_wg[l, 1], ffn_wu[l, 1], ffn_wd[l, 1])
        h_lat = _ffn_call(LAT, h_lat, mods, l, 6, w_ffn, lg(2), lb(2))
        if not last:
            h_ctx = _ffn_call(CTX, h_ctx, mods, l, 6, w_ffn, lg(2), lb(2))
    return h_lat
```

```python
import functools
import math
from typing import NamedTuple

import jax
import jax.numpy as jnp
from jax import lax
from jax.experimental import pallas as pl
from jax.experimental.pallas import tpu as pltpu

F32 = jnp.float32
BF16 = jnp.bfloat16
HIGHEST = lax.Precision.HIGHEST

D_MODEL = 1024
BATCH = 16
SEQ = 2048
DEPTH = 4
GRID_W = 64
CTX_LEN = 256
DN_ALPHA = (2 * DEPTH) ** 0.25
LN_EPS = 1e-6
D_FF = 2816

HY_W = D_MODEL // 2
HY_ORDER = 2
HY_IN = (HY_ORDER + 1) * HY_W
HY_EMB = 33
HY_FO = 64
HY_TARGET = 1e-2
HY_FAST = 0.3
HY_SLOW = 1.5

RW_W = D_MODEL - HY_W
RW_N = 64
RW_H = RW_W // RW_N
RW_DECAY_LORA = 64
RW_A_LORA = 64
RW_GATE_LORA = 128
RW_GN_EPS = 64e-5
RW_IN = 3 * RW_W + 2 * RW_DECAY_LORA + 2 * RW_A_LORA + RW_GATE_LORA
EV_IN = HY_IN + RW_IN

GLA_H = 4
GLA_DK = D_MODEL // 2
GLA_DV = D_MODEL
GLA_HK = GLA_DK // GLA_H
GLA_HV = GLA_DV // GLA_H
GLA_GATE_LORA = 16
GLA_NORMALIZER = 16.0
GLA_CHUNK = 64
GLA_EPS = 1e-5

LANES = 128
SUBLANES = 8

MOD_ROWS = 24
TF = 256
NF = D_FF // TF
SCAN_TB = 32
BH = BATCH * RW_H
HY_S = 256
HY_CT = 256
HY_MAC_ROWS = 16
VMEM_LIMIT = 56 * 1024 * 1024


class _Stream(NamedTuple):
    groups: int
    rows: int
    tm: int
    ctx: bool

    @property
    def grid(self):
        return (self.groups, self.rows // self.tm)


LAT = _Stream(BATCH, SEQ, 512, False)
CTX = _Stream(BATCH // 2, 2 * CTX_LEN, 512, True)
LAT_SEG = _Stream(BATCH, SEQ, 256, False)
CTX_SEG = _Stream(BATCH, CTX_LEN, 256, True)


def _pair_ctx(a):
    return a.reshape(BATCH // 2, 2 * CTX_LEN, a.shape[-1])


def _unpair_ctx(a):
    return a.reshape(BATCH, CTX_LEN, a.shape[-1])


def _cparams(n_axes):
    return pltpu.CompilerParams(dimension_semantics=("arbitrary",) * n_axes,
                                vmem_limit_bytes=VMEM_LIMIT)


def _layer_norm_rows(r, g, b):
    mu = jnp.mean(r, axis=-1, keepdims=True)
    xc = r - mu
    var = jnp.mean(xc * xc, axis=-1, keepdims=True)
    return xc * lax.rsqrt(var + LN_EPS) * g + b


def _silu(x):
    return x * jax.nn.sigmoid(x)


def _split_bf16(x):
    hi = x.astype(BF16)
    lo = (x - hi.astype(F32)).astype(BF16)
    return hi, lo


def _dot(a, b):
    return jnp.dot(a, b, preferred_element_type=F32)


def _dot_f32(a, b):
    return jnp.dot(a, b, preferred_element_type=F32, precision=HIGHEST)


def _dot3(a_hi, a_lo, b_hi, b_lo):
    return _dot(a_hi, b_hi) + (_dot(a_lo, b_hi) + _dot(a_hi, b_lo))


def _dot_x3(a, b):
    return _dot3(*_split_bf16(a), *_split_bf16(b))


def _mod_spec(layer, st):
    def index(g, t):
        return (layer, BATCH if st.ctx else g, 0, 0)
    return pl.BlockSpec((None, None, 9, D_MODEL), index)


def _tok_spec(st, width, col=0):
    return pl.BlockSpec((None, st.tm, width), lambda g, t: (g, t, col))


def _const_spec(shape):
    nd = len(shape)
    return pl.BlockSpec(shape, lambda g, t: (0,) * nd)


def _tok_shape(st, width):
    return jax.ShapeDtypeStruct((st.groups, st.rows, width), F32)


def _ada_kernel(s_ref, w_ref, b_ref, o_ref):
    s = _silu(s_ref[...])
    o_ref[...] = _dot_f32(s, w_ref[...]) + b_ref[...]


def _ada_call(s, ada_w, ada_b):
    tn = 2304
    n_blk = 9 * D_MODEL // tn
    return pl.pallas_call(
        _ada_kernel,
        grid=(DEPTH, n_blk),
        in_specs=[
            pl.BlockSpec((MOD_ROWS, D_MODEL), lambda l, j: (0, 0)),
            pl.BlockSpec((None, D_MODEL, tn), lambda l, j: (l, 0, j)),
            pl.BlockSpec((None, 1, tn), lambda l, j: (l, 0, j)),
        ],
        out_specs=pl.BlockSpec((None, MOD_ROWS, tn), lambda l, j: (l, 0, j)),
        out_shape=jax.ShapeDtypeStruct((DEPTH, MOD_ROWS, 9 * D_MODEL), F32),
        compiler_params=_cparams(2),
        name="ada_mod",
    )(s, ada_w, ada_b.reshape(DEPTH, 1, 9 * D_MODEL))


def _ffn_kernel(h_ref, m_ref, wg_ref, wu_ref, wd_ref, g_ref, b_ref, o_ref, *, mi):
    h = h_ref[...]
    shift = m_ref[mi:mi + 1, :]
    scale = m_ref[mi + 1:mi + 2, :]
    gate = m_ref[mi + 2:mi + 3, :]
    hm = (h * (1.0 + scale) + shift).astype(BF16)
    acc = jnp.zeros(h.shape, F32)
    for j in range(NF):
        a = _silu(_dot(hm, wg_ref[j])) * _dot(hm, wu_ref[j])
        acc = acc + _dot(a.astype(BF16), wd_ref[j])
    r = DN_ALPHA * h + (0.5 * gate) * acc
    o_ref[...] = _layer_norm_rows(r, g_ref[...], b_ref[...])


def _ffn_call(st, h, mods, layer, mi, weights, ln_g, ln_b):
    wg, wu, wd = weights
    return pl.pallas_call(
        functools.partial(_ffn_kernel, mi=mi),
        grid=st.grid,
        in_specs=[
            _tok_spec(st, D_MODEL),
            _mod_spec(layer, st),
            _const_spec((NF, D_MODEL, TF)),
            _const_spec((NF, D_MODEL, TF)),
            _const_spec((NF, TF, D_MODEL)),
            _const_spec((1, D_MODEL)),
            _const_spec((1, D_MODEL)),
        ],
        out_specs=_tok_spec(st, D_MODEL),
        out_shape=_tok_shape(st, D_MODEL),
        compiler_params=_cparams(2),
        name="ffn",
    )(h, mods, wg, wu, wd, ln_g, ln_b)


def _ffn_weights(wg, wu, wd):
    wg = wg.astype(BF16).reshape(D_MODEL, NF, TF).transpose(1, 0, 2)
    wu = wu.astype(BF16).reshape(D_MODEL, NF, TF).transpose(1, 0, 2)
    wd = wd.astype(BF16).reshape(NF, TF, D_MODEL)
    return wg, wu, wd


def _inproj_kernel(h_ref, m_ref, w_ref, *o_refs, widths):
    u = (h_ref[...] * (1.0 + m_ref[4:5, :]) + m_ref[3:4, :]).astype(BF16)
    off = 0
    for o_ref, wd in zip(o_refs, widths):
        o_ref[...] = _dot(u, w_ref[:, off:off + wd])
        off += wd


def _inproj_call(st, h, mods, layer, w, widths):
    n_in = sum(widths)
    return pl.pallas_call(
        functools.partial(_inproj_kernel, widths=widths),
        grid=st.grid,
        in_specs=[_tok_spec(st, D_MODEL), _mod_spec(layer, st), _const_spec((D_MODEL, n_in))],
        out_specs=[_tok_spec(st, wd) for wd in widths],
        out_shape=[_tok_shape(st, wd) for wd in widths],
        compiler_params=_cparams(2),
        name="mixer_in",
    )(h, mods, w)


def _hyena_feature_table(L):
    t = jnp.linspace(0.0, 1.0, L, dtype=F32)[:, None]
    bands = (HY_EMB - 1) // 2
    w = 2.0 * math.pi * jnp.arange(L, dtype=F32)[:, None] / L
    f = jnp.linspace(1e-4, bands - 1, bands, dtype=F32)[None, :]
    z = jnp.concatenate([t, jnp.cos(f * w), -jnp.sin(f * w)], axis=-1)
    z = jnp.pad(z, ((0, 0), (0, LANES - HY_EMB)))
    deltas = jnp.linspace(math.log(HY_TARGET) / HY_SLOW, math.log(HY_TARGET) / HY_FAST, HY_W, dtype=F32)
    neg = (L - jnp.arange(L)) % L
    return z, t, z[neg], t[neg], jnp.abs(deltas)[None, :]


def _hy_filter_kernel(zp_ref, tp_ref, zn_ref, tn_ref, d_ref, w1_ref, b1_ref, f1_ref, w2_ref, b2_ref, f2_ref,
                      w3_ref, g_ref, *, L):
    def hidden(z):
        h = jnp.sin(f1_ref[...] * (_dot_f32(z, w1_ref[...]) + b1_ref[...]))
        return jnp.sin(f2_ref[...] * (_dot_f32(h, w2_ref[...]) + b2_ref[...]))

    h_pos = hidden(zp_ref[...])
    h_neg = hidden(zn_ref[...])
    win_pos = jnp.exp(-tp_ref[...] * d_ref[...])
    win_neg = jnp.exp(-tn_ref[...] * d_ref[...])
    rows = lax.broadcasted_iota(jnp.int32, win_neg.shape, 0)
    for o in range(HY_ORDER):
        base = o * 2 * HY_W
        g_ref[o, L:2 * L, :] = _dot_f32(h_pos, w3_ref[:, base:base + HY_W]) * win_pos
        h_bwd = _dot_f32(h_neg, w3_ref[:, base + HY_W:base + 2 * HY_W]) * win_neg
        g_ref[o, 0:L, :] = jnp.where(rows == 0, 0.0, h_bwd)


def _hy_filter_call(L, w1, b1, f1, w2, b2, f2, w3):
    zp, tp, zn, tn, deltas = _hyena_feature_table(L)
    w1p = jnp.pad(w1, ((0, LANES - HY_EMB), (0, 0)))
    return pl.pallas_call(
        functools.partial(_hy_filter_kernel, L=L),
        out_shape=jax.ShapeDtypeStruct((HY_ORDER, 2 * L, HY_W), F32),
        compiler_params=pltpu.CompilerParams(vmem_limit_bytes=VMEM_LIMIT),
        name="hyena_filter",
    )(zp, tp, zn, tn, deltas, w1p, b1[None, :], f1[None, :], w2, b2[None, :], f2[None, :], w3)


def _block_dft_tables():
    s = HY_S
    n = 2 * s
    f = jnp.arange(s, dtype=jnp.int32)[:, None]

    def trig(cols):
        j = jnp.arange(cols, dtype=jnp.int32)[None, :]
        ang = ((f * j) % n).astype(F32) * (2.0 * math.pi / n)
        alt = jnp.where(j % 2 == 0, 1.0, -1.0).astype(F32)
        return j, jnp.cos(ang), jnp.sin(ang), alt

    _, cos, sin, alt = trig(s)
    fwd = jnp.concatenate([cos, jnp.where(f == 0, alt, -sin)], axis=0)
    scale = jnp.where(f == 0, 1.0 / n, 2.0 / n).astype(F32)
    inv = jnp.concatenate([(scale * cos).T, jnp.where(f == 0, alt / n, -scale * sin).T], axis=1)
    j, cos, sin, alt = trig(n)
    sgn = jnp.where(f % 2 == 0, 1.0, -1.0).astype(F32)
    taps = jnp.concatenate([sgn * cos, jnp.where(f == 0, alt, -sgn * sin)], axis=0)
    taps = jnp.where(j == 0, 0.0, taps)
    return _split_bf16(fwd), _split_bf16(inv), _split_bf16(taps)


def _hy_spectrum_kernel(th_ref, tl_ref, lo_ref, hi_ref, k_ref):
    s = HY_S
    a_hi, a_lo = _split_bf16(lo_ref[...])
    b_hi, b_lo = _split_bf16(hi_ref[...])
    k_ref[...] = (_dot3(th_ref[:, :s], tl_ref[:, :s], a_hi, a_lo)
                  + _dot3(th_ref[:, s:], tl_ref[:, s:], b_hi, b_lo))


def _hy_spectrum_call(L, taps_tab, taps):
    s = HY_S
    nd = 2 * (L // s) - 1
    th, tl = taps_tab
    return pl.pallas_call(
        _hy_spectrum_kernel,
        grid=(HY_ORDER, nd),
        in_specs=[
            pl.BlockSpec((2 * s, 2 * s), lambda o, d: (0, 0)),
            pl.BlockSpec((2 * s, 2 * s), lambda o, d: (0, 0)),
            pl.BlockSpec((None, s, HY_W), lambda o, d: (o, d, 0)),
            pl.BlockSpec((None, s, HY_W), lambda o, d: (o, d + 1, 0)),
        ],
        out_specs=pl.BlockSpec((None, None, 2 * s, HY_W), lambda o, d: (o, d, 0, 0)),
        out_shape=jax.ShapeDtypeStruct((HY_ORDER, nd, 2 * s, HY_W), F32),
        compiler_params=_cparams(2),
        name="hyena_spectrum",
    )(th, tl, taps, taps)


def _conv3_rows(z, w, b):
    n = z.shape[0]
    rows = lax.broadcasted_iota(jnp.int32, z.shape, 0)
    prev = jnp.where(rows == 0, 0.0, pltpu.roll(z, 1, 0))
    nxt = jnp.where(rows == n - 1, 0.0, pltpu.roll(z, n - 1, 0))
    return prev * w[0:1, :] + z * w[1:2, :] + nxt * w[2:3, :] + b


def _hy_conv_kernel(u_ref, x_ref, cwu_ref, cbu_ref, cwx_ref, cbx_ref, skip_ref, k_ref, fwd_ref, inv_ref,
                    o_ref, u_sc, xf_ref, yf_ref, *, m, conv_u):
    s = HY_S
    ct = o_ref.shape[-1]
    if conv_u:
        u_sc[...] = _conv3_rows(u_ref[...], cwu_ref[...], cbu_ref[...])
        src = u_sc
    else:
        src = u_ref
    o_ref[...] = _conv3_rows(x_ref[...], cwx_ref[...], cbx_ref[...])

    for jb in range(m):
        xf_ref[jb] = _dot(fwd_ref[...], src[jb * s:(jb + 1) * s, :].astype(BF16))

    for ib in range(m):
        def chunk(r, carry):
            re = pl.ds(pl.multiple_of(r * HY_MAC_ROWS, HY_MAC_ROWS), HY_MAC_ROWS)
            im = pl.ds(pl.multiple_of(s + r * HY_MAC_ROWS, HY_MAC_ROWS), HY_MAC_ROWS)
            acc_re = jnp.zeros((HY_MAC_ROWS, ct), F32)
            acc_im = jnp.zeros((HY_MAC_ROWS, ct), F32)
            for jb in range(m):
                d = ib - jb + m - 1
                x_re, x_im = xf_ref[jb, re, :], xf_ref[jb, im, :]
                k_re, k_im = k_ref[d, re, :], k_ref[d, im, :]
                acc_re = acc_re + (x_re * k_re - x_im * k_im)
                acc_im = acc_im + (x_re * k_im + x_im * k_re)
            yf_ref[ib, re, :] = acc_re
            yf_ref[ib, im, :] = acc_im
            return carry

        lax.fori_loop(0, s // HY_MAC_ROWS, chunk, 0)
        dc = jnp.zeros((1, ct), F32)
        ny = jnp.zeros((1, ct), F32)
        for jb in range(m):
            d = ib - jb + m - 1
            dc = dc + xf_ref[jb, 0:1, :] * k_ref[d, 0:1, :]
            ny = ny + xf_ref[jb, s:s + 1, :] * k_ref[d, s:s + 1, :]
        yf_ref[ib, 0:1, :] = dc
        yf_ref[ib, s:s + 1, :] = ny

    skip = skip_ref[...]
    for ib in range(m):
        conv = _dot(inv_ref[...], yf_ref[ib].astype(BF16))
        rows = slice(ib * s, (ib + 1) * s)
        o_ref[rows, :] = o_ref[rows, :] * (conv + skip * src[rows, :])


def _hy_conv_call(u, u_part, z_hy, x_part, L, order, conv_u, conv_w, conv_b, skip, spectrum, fwd_tab, inv_tab):
    s = HY_S
    m = L // s
    nd = 2 * m - 1
    ct = HY_CT
    nct = HY_W // ct
    fwd, inv = fwd_tab[0], inv_tab[0]

    def cspec(rows, part):
        return pl.BlockSpec((rows, ct), lambda c, b: (0, part * nct + c))

    tab = lambda shape: pl.BlockSpec(shape, lambda c, b: (0, 0))
    return pl.pallas_call(
        functools.partial(_hy_conv_kernel, m=m, conv_u=conv_u),
        grid=(nct, BATCH),
        in_specs=[
            pl.BlockSpec((None, L, ct), lambda c, b: (b, 0, u_part * nct + c)),
            pl.BlockSpec((None, L, ct), lambda c, b: (b, 0, x_part * nct + c)),
            cspec(3, u_part), cspec(1, u_part), cspec(3, x_part), cspec(1, x_part),
            pl.BlockSpec((None, 1, ct), lambda c, b: (order, 0, c)),
            pl.BlockSpec((None, nd, 2 * s, ct), lambda c, b: (order, 0, 0, c)),
            tab((2 * s, s)), tab((s, 2 * s)),
        ],
        out_specs=pl.BlockSpec((None, L, ct), lambda c, b: (b, 0, c)),
        out_shape=jax.ShapeDtypeStruct((BATCH, L, HY_W), F32),
        scratch_shapes=[
            pltpu.VMEM((L, ct), F32),
            pltpu.VMEM((m, 2 * s, ct), F32),
            pltpu.VMEM((m, 2 * s, ct), F32),
        ],
        compiler_params=_cparams(2),
        name="hyena_conv",
    )(u, z_hy, conv_w, conv_b, conv_w, conv_b, skip[:, None, :], spectrum, fwd, inv)


def _hyena(z_hy, L, tables, conv_w, conv_b, f_w1, f_b1, f_freq1, f_w2, f_b2, f_freq2, f_w3, skip):
    fwd_tab, inv_tab, taps_tab = tables
    taps = _hy_filter_call(L, f_w1, f_b1, f_freq1, f_w2, f_b2, f_freq2, f_w3)
    spectrum = _hy_spectrum_call(L, taps_tab, taps)
    y1 = _hy_conv_call(z_hy, 0, z_hy, 1, L, 0, True, conv_w, conv_b, skip, spectrum, fwd_tab, inv_tab)
    return _hy_conv_call(y1, 0, z_hy, 2, L, 1, False, conv_w, conv_b, skip, spectrum, fwd_tab, inv_tab)


def _head_sum_matrix(width, head):
    i = jnp.arange(width)[:, None] // head
    j = jnp.arange(width)[None, :] // head
    return (i == j).astype(BF16)


def _group_sum(x, ones_ref):
    hi, lo = _split_bf16(x)
    return _dot(hi, ones_ref[...]) + _dot(lo, ones_ref[...])


def _rw_prep_kernel(z_ref, zp_ref, zn_ref, mu_ref, kk_ref, ka_ref, rk_ref, w0_ref, a0_ref, wup_ref, aup_ref,
                    gup_ref, ones_ref,
                    r_ref, v_ref, nkk_ref, g_ref, bonus_ref, wf_ref, kf_ref, af_ref, wb_ref, kb_ref, ab_ref):
    t = pl.program_id(1)
    z = z_ref[...]
    tm = z.shape[0]
    prev_row = jnp.where(t == 0, 0.0, zp_ref[SUBLANES - 1:SUBLANES, :])
    next_row = jnp.where(t == pl.num_programs(1) - 1, 0.0, zn_ref[0:1, :])
    rows = lax.broadcasted_iota(jnp.int32, z.shape, 0)
    prev = jnp.where(rows == 0, prev_row, pltpu.roll(z, 1, 0))
    nxt = jnp.where(rows == tm - 1, next_row, pltpu.roll(z, tm - 1, 0))
    z = z + mu_ref[...] * (0.5 * (prev + nxt) - z)

    W = RW_W
    r = z[:, 0:W]
    k = z[:, W:2 * W]
    v = z[:, 2 * W:3 * W]
    o = 3 * W
    w_lora = _dot_x3(jnp.tanh(z[:, o:o + 2 * RW_DECAY_LORA]), wup_ref[...])
    o += 2 * RW_DECAY_LORA
    a_lora = _dot_x3(z[:, o:o + 2 * RW_A_LORA], aup_ref[...])
    o += 2 * RW_A_LORA
    g = _dot_x3(jax.nn.sigmoid(z[:, o:o + RW_GATE_LORA]), gup_ref[...])

    kk = k * kk_ref[...]
    norm = jnp.sqrt(_group_sum(kk * kk, ones_ref))
    kk = kk / jnp.maximum(norm, 1e-12)

    k_sum = jnp.zeros_like(k)
    for d, (w_ref, kd_ref, ad_ref) in enumerate(((wf_ref, kf_ref, af_ref), (wb_ref, kb_ref, ab_ref))):
        y = -(w0_ref[d:d + 1, :] + w_lora[:, d * W:(d + 1) * W])
        softplus = jnp.maximum(y, 0.0) + jnp.log(1.0 + jnp.exp(-jnp.abs(y)))
        w_log = -softplus - 0.5
        a = jax.nn.sigmoid(a0_ref[d:d + 1, :] + a_lora[:, d * W:(d + 1) * W])
        kd = k * (1.0 + (a - 1.0) * ka_ref[...])
        w_ref[...] = jnp.exp(-jnp.exp(w_log))
        kd_ref[...] = kd
        ad_ref[...] = kk * a
        k_sum = k_sum + kd

    r_ref[...] = r
    v_ref[...] = v
    nkk_ref[...] = -kk
    g_ref[...] = g
    bonus_ref[...] = _group_sum(r * k_sum * rk_ref[...], ones_ref) * v


def _rw_prep_call(st, z_rw, params):
    W = RW_W
    blocks_per_tile = st.tm // SUBLANES
    n_row_blocks = st.rows // SUBLANES
    return pl.pallas_call(
        _rw_prep_kernel,
        grid=st.grid,
        in_specs=[
            _tok_spec(st, RW_IN),
            pl.BlockSpec((None, SUBLANES, RW_IN),
                         lambda g, t: (g, jnp.maximum(t * blocks_per_tile - 1, 0), 0)),
            pl.BlockSpec((None, SUBLANES, RW_IN),
                         lambda g, t: (g, jnp.minimum((t + 1) * blocks_per_tile, n_row_blocks - 1), 0)),
            _const_spec((1, RW_IN)),
            _const_spec((1, W)), _const_spec((1, W)), _const_spec((1, W)),
            _const_spec((2, W)), _const_spec((2, W)),
            _const_spec((2 * RW_DECAY_LORA, 2 * W)), _const_spec((2 * RW_A_LORA, 2 * W)),
            _const_spec((RW_GATE_LORA, W)),
            _const_spec((W, W)),
        ],
        out_specs=[_tok_spec(st, W)] * 11,
        out_shape=[_tok_shape(st, W)] * 11,
        compiler_params=_cparams(2),
        name="rwkv_prepare",
    )(z_rw, z_rw, z_rw, *params)


def _rw_prep_params(mu, k_k, k_a, r_k, w0, a0, w_up, a_up, g_up):
    W = RW_W
    zero = jnp.zeros((RW_DECAY_LORA, W), F32)
    wup = jnp.concatenate([jnp.concatenate([w_up[0], zero], 1), jnp.concatenate([zero, w_up[1]], 1)], 0)
    aup = jnp.concatenate([jnp.concatenate([a_up[0], zero], 1), jnp.concatenate([zero, a_up[1]], 1)], 0)
    return (mu[None, :], k_k[None, :], k_a[None, :], r_k.reshape(1, W), w0, a0, wup, aup, g_up,
            _head_sum_matrix(W, RW_N))


def _rw_scan_kernel(r_ref, v_ref, nkk_ref, w_ref, kd_ref, ka_ref, s0_ref, *rest, reverse, accumulate):
    if accumulate:
        oacc_ref, o_ref, sfin_ref, s_ref, sa_ref = rest
    else:
        o_ref, sfin_ref, s_ref, sa_ref = rest
    @pl.when(pl.program_id(0) == 0)
    def _():
        s_ref[...] = s0_ref[...]

    t_first = SCAN_TB - 1 if reverse else 0
    sa = jnp.zeros((RW_N, BH), F32)
    for k in range(RW_N):
        sa = sa + s_ref[k] * nkk_ref[t_first, k:k + 1, :]
    sa_ref[...] = sa

    def step(i, carry):
        t = SCAN_TB - 1 - i if reverse else i
        t_next = jnp.clip(t - 1 if reverse else t + 1, 0, SCAN_TB - 1)
        sa = sa_ref[...]
        vv = v_ref[t]
        out = jnp.zeros((RW_N, BH), F32)
        sa_next = jnp.zeros((RW_N, BH), F32)
        for k in range(RW_N):
            s_k = (s_ref[k] * w_ref[t, k:k + 1, :] + sa * ka_ref[t, k:k + 1, :]
                   + vv * kd_ref[t, k:k + 1, :])
            s_ref[k] = s_k
            out = out + s_k * r_ref[t, k:k + 1, :]
            sa_next = sa_next + s_k * nkk_ref[t_next, k:k + 1, :]
        o_ref[t] = out + oacc_ref[t] if accumulate else out
        sa_ref[...] = sa_next
        return carry

    lax.fori_loop(0, SCAN_TB, step, 0)

    @pl.when(pl.program_id(0) == pl.num_programs(0) - 1)
    def _():
        sfin_ref[...] = s_ref[...]


def _rw_scan_call(r, v, nkk, w, kd, ka, s0, reverse, o_other=None):
    n_blk = r.shape[0] // SCAN_TB
    spec = pl.BlockSpec((SCAN_TB, RW_N, BH), lambda i: ((n_blk - 1 - i) if reverse else i, 0, 0))
    state = pl.BlockSpec((RW_N, RW_N, BH), lambda i: (0, 0, 0))
    extra = [] if o_other is None else [o_other]
    return pl.pallas_call(
        functools.partial(_rw_scan_kernel, reverse=reverse, accumulate=o_other is not None),
        grid=(n_blk,),
        in_specs=[spec] * 6 + [state] + [spec] * len(extra),
        out_specs=[spec, state],
        out_shape=[jax.ShapeDtypeStruct(r.shape, F32), jax.ShapeDtypeStruct((RW_N, RW_N, BH), F32)],
        scratch_shapes=[pltpu.VMEM((RW_N, RW_N, BH), F32), pltpu.VMEM((RW_N, BH), F32)],
        compiler_params=_cparams(1),
        name="rwkv_scan",
    )(r, v, nkk, w, kd, ka, s0, *extra)


def _to_scan(x):
    n = x.shape[1]
    return x.reshape(BATCH, n, RW_H, RW_N).transpose(1, 3, 0, 2).reshape(n, RW_N, BH)


def _from_scan(x):
    n = x.shape[0]
    return x.reshape(n, RW_N, BATCH, RW_H).transpose(2, 0, 3, 1).reshape(BATCH, n, RW_W)


def _rwkv_bidir(prep_ctx, prep_lat):
    outs = {"ctx": None, "lat": None}
    shared = {name: [_to_scan(p[i]) for i in range(3)] for name, p in (("ctx", prep_ctx), ("lat", prep_lat))}
    for d, reverse in enumerate((False, True)):
        state = jnp.zeros((RW_N, RW_N, BH), F32)
        for name, p in (("ctx", prep_ctx), ("lat", prep_lat)):
            dirs = [_to_scan(p[5 + 3 * d + i]) for i in range(3)]
            outs[name], state = _rw_scan_call(*shared[name], *dirs, state, reverse, outs[name])
    return _from_scan(outs["ctx"]), _from_scan(outs["lat"])


def _even_out_kernel(h_ref, m_ref, hy_ref, osum_ref, bonus_ref, g_ref, gng_ref, gnb_ref, ones_ref,
                     why_ref, wrw_ref, lg_ref, lb_ref, o_ref):
    o = osum_ref[...]
    mu = _group_sum(o, ones_ref) * (1.0 / RW_N)
    oc = o - mu
    var = _group_sum(oc * oc, ones_ref) * (1.0 / RW_N)
    y = oc * lax.rsqrt(var + RW_GN_EPS) * gng_ref[...] + gnb_ref[...]
    rw = (y + bonus_ref[...]) * g_ref[...]
    mix = _dot(hy_ref[...].astype(BF16), why_ref[...]) + _dot(rw.astype(BF16), wrw_ref[...])
    r = DN_ALPHA * h_ref[...] + m_ref[5:6, :] * mix
    o_ref[...] = _layer_norm_rows(r, lg_ref[...], lb_ref[...])


def _even_out_call(st, h, mods, layer, hy, o_sum, bonus, g, gn_g, gn_b, w_out, ln_g, ln_b):
    W = RW_W
    return pl.pallas_call(
        _even_out_kernel,
        grid=st.grid,
        in_specs=[
            _tok_spec(st, D_MODEL), _mod_spec(layer, st),
            _tok_spec(st, HY_W), _tok_spec(st, W), _tok_spec(st, W), _tok_spec(st, W),
            _const_spec((1, W)), _const_spec((1, W)), _const_spec((W, W)),
            _const_spec((HY_W, D_MODEL)), _const_spec((W, D_MODEL)),
            _const_spec((1, D_MODEL)), _const_spec((1, D_MODEL)),
        ],
        out_specs=_tok_spec(st, D_MODEL),
        out_shape=_tok_shape(st, D_MODEL),
        compiler_params=_cparams(2),
        name="even_out",
    )(h, mods, hy, o_sum, bonus, g, gn_g[None, :], gn_b[None, :], _head_sum_matrix(W, RW_N),
      w_out[:HY_W], w_out[HY_W:], ln_g, ln_b)


GLA_GPAD = LANES
OD_WIDTHS = (GLA_DK, GLA_DK, GLA_DV, GLA_DV, GLA_GPAD)


def _odd_inproj_kernel(h_ref, m_ref, w_ref, gup_ref, gb_ref, q_ref, k_ref, v_ref, og_ref, gf_ref, gb_out_ref):
    u = (h_ref[...] * (1.0 + m_ref[4:5, :]) + m_ref[3:4, :]).astype(BF16)
    off = 0
    for o_ref, wd in zip((q_ref, k_ref, v_ref, og_ref), OD_WIDTHS[:4]):
        o_ref[...] = _dot(u, w_ref[:, off:off + wd])
        off += wd
    gd = _dot(u, w_ref[:, off:off + GLA_GPAD])
    logit = _dot_x3(gd, gup_ref[...]) + gb_ref[...]
    log_sig = jnp.minimum(logit, 0.0) - jnp.log(1.0 + jnp.exp(-jnp.abs(logit)))
    g = log_sig * (1.0 / GLA_NORMALIZER)
    gf_ref[...] = g[:, :GLA_DK]
    gb_out_ref[...] = g[:, GLA_DK:]


def _odd_inproj_call(st, h, mods, layer, w, gup, g_b):
    widths = OD_WIDTHS[:4] + (GLA_DK, GLA_DK)
    return pl.pallas_call(
        _odd_inproj_kernel,
        grid=st.grid,
        in_specs=[_tok_spec(st, D_MODEL), _mod_spec(layer, st), _const_spec((D_MODEL, sum(OD_WIDTHS))),
                  _const_spec((GLA_GPAD, 2 * GLA_DK)), _const_spec((1, 2 * GLA_DK))],
        out_specs=[_tok_spec(st, wd) for wd in widths],
        out_shape=[_tok_shape(st, wd) for wd in widths],
        compiler_params=_cparams(2),
        name="mixer_in_gla",
    )(h, mods, w, gup, g_b)


def _gla_chunk(q_ref, k_ref, v_ref, g_ref, o_ref, s_ref, reverse):
    C = GLA_CHUNK
    g = g_ref[...]
    ri = lax.broadcasted_iota(jnp.int32, (C, C), 0)
    ci = lax.broadcasted_iota(jnp.int32, (C, C), 1)
    causal = (ci >= ri) if reverse else (ci <= ri)
    b = _dot_f32(causal.astype(F32), g)
    mid, end = (C - 1 - C // 2, 0) if reverse else (C // 2, C - 1)
    b_mid = b[mid:mid + 1, :]
    b_end = b[end:end + 1, :]
    q = q_ref[...] * (GLA_HK ** -0.5)
    k = k_ref[...]
    q_intra = (q * jnp.exp(b - b_mid)).astype(BF16)
    k_intra = (k * jnp.exp(b_mid - b)).astype(BF16)
    q_in = (q * jnp.exp(b)).astype(BF16)
    k_out = (k * jnp.exp(b_end - b)).astype(BF16)
    d_end = jnp.exp(b_end)
    v = v_ref[...].astype(BF16)
    nt = (((1,), (1,)), ((), ()))
    tn = (((0,), (0,)), ((), ()))
    for hd in range(GLA_H):
        ks = slice(hd * GLA_HK, (hd + 1) * GLA_HK)
        vs = slice(hd * GLA_HV, (hd + 1) * GLA_HV)
        scores = lax.dot_general(q_intra[:, ks], k_intra[:, ks], nt, preferred_element_type=F32)
        scores = jnp.where(causal, scores, 0.0).astype(BF16)
        state = s_ref[hd]
        o_h = _dot(scores, v[:, vs]) + lax.dot_general(q_in[:, ks], state.astype(BF16), nt,
                                                       preferred_element_type=F32)
        o_ref[:, vs] = o_h
        s_ref[hd] = state * d_end[:, ks] + lax.dot_general(v[:, vs], k_out[:, ks], tn,
                                                           preferred_element_type=F32)


def _gla_kernel(qf_ref, kf_ref, vf_ref, gf_ref, qb_ref, kb_ref, vb_ref, gb_ref, s0f_ref, s0b_ref,
                of_ref, ob_ref, sff_ref, sfb_ref, sf_sc, sb_sc):
    @pl.when(pl.program_id(1) == 0)
    def _():
        sf_sc[...] = s0f_ref[...]
        sb_sc[...] = s0b_ref[...]

    _gla_chunk(qf_ref, kf_ref, vf_ref, gf_ref, of_ref, sf_sc, False)
    _gla_chunk(qb_ref, kb_ref, vb_ref, gb_ref, ob_ref, sb_sc, True)

    @pl.when(pl.program_id(1) == pl.num_programs(1) - 1)
    def _():
        sff_ref[...] = sf_sc[...]
        sfb_ref[...] = sb_sc[...]


def _gla_call(q, k, v, g_f, g_b, s0_f, s0_b):
    C = GLA_CHUNK
    n_chunk = q.shape[1] // C

    def spec(width, reverse):
        return pl.BlockSpec((None, C, width), lambda b, i: (b, (n_chunk - 1 - i) if reverse else i, 0))

    state = pl.BlockSpec((None, GLA_H, GLA_HV, GLA_HK), lambda b, i: (b, 0, 0, 0))
    ins = lambda reverse: [spec(GLA_DK, reverse), spec(GLA_DK, reverse), spec(GLA_DV, reverse),
                           spec(GLA_DK, reverse)]
    o_shape = jax.ShapeDtypeStruct((BATCH, q.shape[1], GLA_DV), F32)
    s_shape = jax.ShapeDtypeStruct((BATCH, GLA_H, GLA_HV, GLA_HK), F32)
    return pl.pallas_call(
        _gla_kernel,
        grid=(BATCH, n_chunk),
        in_specs=ins(False) + ins(True) + [state, state],
        out_specs=[spec(GLA_DV, False), spec(GLA_DV, True), state, state],
        out_shape=[o_shape, o_shape, s_shape, s_shape],
        scratch_shapes=[pltpu.VMEM((GLA_H, GLA_HV, GLA_HK), F32)] * 2,
        compiler_params=_cparams(2),
        name="gla_scan",
    )(q, k, v, g_f, q, k, v, g_b, s0_f, s0_b)


def _odd_out_kernel(h_ref, m_ref, of_ref, ob_ref, og_ref, ng_ref, w_ref, lg_ref, lb_ref, o_ref):
    o = of_ref[...] + ob_ref[...]
    og = og_ref[...]
    parts = []
    for hd in range(GLA_H):
        o_h = o[:, hd * GLA_HV:(hd + 1) * GLA_HV]
        ms = jnp.mean(o_h * o_h, axis=-1, keepdims=True)
        parts.append(o_h * lax.rsqrt(ms + GLA_EPS) * ng_ref[...])
    y = jnp.concatenate(parts, axis=-1) * _silu(og)
    mix = _dot(y.astype(BF16), w_ref[...])
    r = DN_ALPHA * h_ref[...] + m_ref[5:6, :] * mix
    o_ref[...] = _layer_norm_rows(r, lg_ref[...], lb_ref[...])


def _odd_out_call(st, h, mods, layer, o_f, o_b, og, norm_g, w_out, ln_g, ln_b):
    return pl.pallas_call(
        _odd_out_kernel,
        grid=st.grid,
        in_specs=[
            _tok_spec(st, D_MODEL), _mod_spec(layer, st),
            _tok_spec(st, GLA_DV), _tok_spec(st, GLA_DV), _tok_spec(st, GLA_DV),
            _const_spec((1, GLA_HV)), _const_spec((GLA_DV, D_MODEL)),
            _const_spec((1, D_MODEL)), _const_spec((1, D_MODEL)),
        ],
        out_specs=_tok_spec(st, D_MODEL),
        out_shape=_tok_shape(st, D_MODEL),
        compiler_params=_cparams(2),
        name="odd_out",
    )(h, mods, o_f, o_b, og, norm_g[None, :], w_out, ln_g, ln_b)


def _raster_to_columns(a):
    return a.reshape(BATCH, SEQ // GRID_W, GRID_W, a.shape[-1]).swapaxes(1, 2).reshape(a.shape)


def _columns_to_raster(a):
    return a.reshape(BATCH, GRID_W, SEQ // GRID_W, a.shape[-1]).swapaxes(1, 2).reshape(a.shape)


def kernel(x, c, ctx, c_ctx, ada_w, ada_b, ln_g, ln_b, ffn_wg, ffn_wu, ffn_wd, ev_w_in, ev_w_out, hy_conv_w, hy_conv_b, hy_f_w1, hy_f_b1, hy_f_freq1, hy_f_w2, hy_f_b2, hy_f_freq2, hy_f_w3, hy_skip, rw_mu, rw_w0, rw_w_up, rw_a0, rw_a_up, rw_g_up, rw_k_k, rw_k_a, rw_r_k, rw_gn_g, rw_gn_b, od_w_in, od_w_out, gla_g_up, gla_g_b, gla_norm_g):
    h_lat, h_ctx = x, _pair_ctx(ctx)
    s = jnp.concatenate([c, c_ctx[None, :], jnp.zeros((MOD_ROWS - BATCH - 1, D_MODEL), F32)], axis=0)
    mods = _ada_call(s, ada_w, ada_b).reshape(DEPTH, MOD_ROWS, 9, D_MODEL)
    dft_tables = _block_dft_tables()

    for l in range(DEPTH):
        last = l == DEPTH - 1
        lg = lambda i: ln_g[l, i][None, :]
        lb = lambda i: ln_b[l, i][None, :]
        w_ffn = _ffn_weights(ffn_wg[l, 0], ffn_wu[l, 0], ffn_wd[l, 0])
        h_lat = _ffn_call(LAT, h_lat, mods, l, 0, w_ffn, lg(0), lb(0))
        h_ctx = _ffn_call(CTX, h_ctx, mods, l, 0, w_ffn, lg(0), lb(0))

        if l % 2 == 0:
            e = l // 2
            w_in = ev_w_in[e].astype(BF16)
            w_out = ev_w_out[e].astype(BF16)
            hy_params = (hy_conv_w[e], hy_conv_b[e][None, :], hy_f_w1[e], hy_f_b1[e], hy_f_freq1[e], hy_f_w2[e],
                         hy_f_b2[e], hy_f_freq2[e], hy_f_w3[e], hy_skip[e])
            rw_params = _rw_prep_params(rw_mu[e], rw_k_k[e], rw_k_a[e], rw_r_k[e], rw_w0[e], rw_a0[e],
                                        rw_w_up[e], rw_a_up[e], rw_g_up[e])
            zl_hy, zl_rw = _inproj_call(LAT, h_lat, mods, l, w_in, (HY_IN, RW_IN))
            zc_hy, zc_rw = _inproj_call(CTX, h_ctx, mods, l, w_in, (HY_IN, RW_IN))
            hy_lat = _hyena(zl_hy, SEQ, dft_tables, *hy_params)
            hy_ctx = _hyena(_unpair_ctx(zc_hy), CTX_LEN, dft_tables, *hy_params)
            prep_lat = _rw_prep_call(LAT_SEG, zl_rw, rw_params)
            prep_ctx = _rw_prep_call(CTX_SEG, _unpair_ctx(zc_rw), rw_params)
            o_ctx, o_lat = _rwkv_bidir(prep_ctx, prep_lat)
            h_lat = _even_out_call(LAT, h_lat, mods, l, hy_lat, o_lat, prep_lat[4], prep_lat[3],
                                   rw_gn_g[e], rw_gn_b[e], w_out, lg(1), lb(1))
            if not last:
                h_ctx = _even_out_call(CTX, h_ctx, mods, l, _pair_ctx(hy_ctx), _pair_ctx(o_ctx),
                                       _pair_ctx(prep_ctx[4]), _pair_ctx(prep_ctx[3]),
                                       rw_gn_g[e], rw_gn_b[e], w_out, lg(1), lb(1))
        else:
            o = l // 2
            w = od_w_in[o]
            n_qkv = 2 * GLA_DK + GLA_DV
            n_gate = 2 * GLA_GATE_LORA
            w_in = jnp.concatenate([w[:, :n_qkv], w[:, n_qkv + n_gate:], w[:, n_qkv:n_qkv + n_gate],
                                    jnp.zeros((D_MODEL, GLA_GPAD - n_gate), F32)], axis=1).astype(BF16)
            w_out = od_w_out[o].astype(BF16)
            pad = jnp.zeros((GLA_GPAD - n_gate, GLA_DK), F32)
            zero = jnp.zeros((GLA_GATE_LORA, GLA_DK), F32)
            gup = jnp.concatenate([jnp.concatenate([gla_g_up[o, 0], zero, pad], axis=0),
                                   jnp.concatenate([zero, gla_g_up[o, 1], pad], axis=0)], axis=1)
            g_bias = gla_g_b[o].reshape(1, 2 * GLA_DK)
            h_lat = _raster_to_columns(h_lat)
            ql, kl, vl, ogl, gfl, gbl = _odd_inproj_call(LAT, h_lat, mods, l, w_in, gup, g_bias)
            qc, kc, vc, ogc, gfc, gbc = [_unpair_ctx(a)
                                         for a in _odd_inproj_call(CTX, h_ctx, mods, l, w_in, gup, g_bias)]
            zero_state = jnp.zeros((BATCH, GLA_H, GLA_HV, GLA_HK), F32)
            oc_f, oc_b, s_f, s_b = _gla_call(qc, kc, vc, gfc, gbc, zero_state, zero_state)
            ol_f, ol_b, _, _ = _gla_call(ql, kl, vl, gfl, gbl, s_f, s_b)
            h_lat = _odd_out_call(LAT, h_lat, mods, l, ol_f, ol_b, ogl, gla_norm_g[o], w_out, lg(1), lb(1))
            h_lat = _columns_to_raster(h_lat)
            if not last:
                h_ctx = _odd_out_call(CTX, h_ctx, mods, l, _pair_ctx(oc_f), _pair_ctx(oc_b),
                                      _pair_ctx(ogc), gla_norm_g[o], w_out, lg(1), lb(1))

        w_ffn = _ffn_weights(ffn_wg[l, 1], ffn_wu[l, 1], ffn_wd[l, 1])
        h_lat = _ffn_call(LAT, h_lat, mods, l, 6, w_ffn, lg(2), lb(2))
        if not last:
            h_ctx = _ffn_call(CTX, h_ctx, mods, l, 6, w_ffn, lg(2), lb(2))
    return h_lat
```

```python
import functools
import math
from typing import NamedTuple

import jax
import jax.numpy as jnp
from jax import lax
from jax.experimental import pallas as pl
from jax.experimental.pallas import tpu as pltpu

F32 = jnp.float32
BF16 = jnp.bfloat16
HIGHEST = lax.Precision.HIGHEST

D_MODEL = 1024
BATCH = 16
SEQ = 2048
DEPTH = 4
GRID_W = 64
CTX_LEN = 256
DN_ALPHA = (2 * DEPTH) ** 0.25
LN_EPS = 1e-6
D_FF = 2816

HY_W = D_MODEL // 2
HY_ORDER = 2
HY_IN = (HY_ORDER + 1) * HY_W
HY_EMB = 33
HY_FO = 64
HY_TARGET = 1e-2
HY_FAST = 0.3
HY_SLOW = 1.5

RW_W = D_MODEL - HY_W
RW_N = 64
RW_H = RW_W // RW_N
RW_DECAY_LORA = 64
RW_A_LORA = 64
RW_GATE_LORA = 128
RW_GN_EPS = 64e-5
RW_IN = 3 * RW_W + 2 * RW_DECAY_LORA + 2 * RW_A_LORA + RW_GATE_LORA
EV_IN = HY_IN + RW_IN

GLA_H = 4
GLA_DK = D_MODEL // 2
GLA_DV = D_MODEL
GLA_HK = GLA_DK // GLA_H
GLA_HV = GLA_DV // GLA_H
GLA_GATE_LORA = 16
GLA_NORMALIZER = 16.0
GLA_CHUNK = 64
GLA_EPS = 1e-5

LANES = 128
SUBLANES = 8

MOD_ROWS = 24
TF = 256
NF = D_FF // TF
SCAN_TB = 32
BH = BATCH * RW_H
HY_S = 512
HY_CT = 256
HY_MAC_ROWS = 32
VMEM_LIMIT = 56 * 1024 * 1024


class _Stream(NamedTuple):
    groups: int
    rows: int
    tm: int
    ctx: bool

    @property
    def grid(self):
        return (self.groups, self.rows // self.tm)


LAT = _Stream(BATCH, SEQ, 512, False)
CTX = _Stream(BATCH // 2, 2 * CTX_LEN, 512, True)
LAT_SEG = _Stream(BATCH, SEQ, 256, False)
CTX_SEG = _Stream(BATCH, CTX_LEN, 256, True)


def _pair_ctx(a):
    return a.reshape(BATCH // 2, 2 * CTX_LEN, a.shape[-1])


def _unpair_ctx(a):
    return a.reshape(BATCH, CTX_LEN, a.shape[-1])


def _cparams(n_axes):
    return pltpu.CompilerParams(dimension_semantics=("arbitrary",) * n_axes,
                                vmem_limit_bytes=VMEM_LIMIT)


def _layer_norm_rows(r, g, b):
    mu = jnp.mean(r, axis=-1, keepdims=True)
    xc = r - mu
    var = jnp.mean(xc * xc, axis=-1, keepdims=True)
    return xc * lax.rsqrt(var + LN_EPS) * g + b


def _silu(x):
    return x * jax.nn.sigmoid(x)


def _split_bf16(x):
    hi = x.astype(BF16)
    lo = (x - hi.astype(F32)).astype(BF16)
    return hi, lo


def _dot(a, b):
    return jnp.dot(a, b, preferred_element_type=F32)


def _dot_f32(a, b):
    return jnp.dot(a, b, preferred_element_type=F32, precision=HIGHEST)


def _dot3(a_hi, a_lo, b_hi, b_lo):
    return _dot(a_hi, b_hi) + (_dot(a_lo, b_hi) + _dot(a_hi, b_lo))


def _dot_x3(a, b):
    return _dot3(*_split_bf16(a), *_split_bf16(b))


def _mod_spec(layer, st):
    def index(g, t):
        return (layer, BATCH if st.ctx else g, 0, 0)
    return pl.BlockSpec((None, None, 9, D_MODEL), index)


def _tok_spec(st, width, col=0):
    return pl.BlockSpec((None, st.tm, width), lambda g, t: (g, t, col))


def _const_spec(shape):
    nd = len(shape)
    return pl.BlockSpec(shape, lambda g, t: (0,) * nd)


def _tok_shape(st, width):
    return jax.ShapeDtypeStruct((st.groups, st.rows, width), F32)


def _ada_kernel(s_ref, w_ref, b_ref, o_ref):
    s = _silu(s_ref[...])
    o_ref[...] = _dot_f32(s, w_ref[...]) + b_ref[...]


def _ada_call(s, ada_w, ada_b):
    tn = 2304
    n_blk = 9 * D_MODEL // tn
    return pl.pallas_call(
        _ada_kernel,
        grid=(DEPTH, n_blk),
        in_specs=[
            pl.BlockSpec((MOD_ROWS, D_MODEL), lambda l, j: (0, 0)),
            pl.BlockSpec((None, D_MODEL, tn), lambda l, j: (l, 0, j)),
            pl.BlockSpec((None, 1, tn), lambda l, j: (l, 0, j)),
        ],
        out_specs=pl.BlockSpec((None, MOD_ROWS, tn), lambda l, j: (l, 0, j)),
        out_shape=jax.ShapeDtypeStruct((DEPTH, MOD_ROWS, 9 * D_MODEL), F32),
        compiler_params=_cparams(2),
        name="ada_mod",
    )(s, ada_w, ada_b.reshape(DEPTH, 1, 9 * D_MODEL))


def _ffn_kernel(h_ref, m_ref, wg_ref, wu_ref, wd_ref, g_ref, b_ref, o_ref, *, mi):
    h = h_ref[...]
    shift = m_ref[mi:mi + 1, :]
    scale = m_ref[mi + 1:mi + 2, :]
    gate = m_ref[mi + 2:mi + 3, :]
    hm = (h * (1.0 + scale) + shift).astype(BF16)
    acc = jnp.zeros(h.shape, F32)
    for j in range(NF):
        a = _silu(_dot(hm, wg_ref[j])) * _dot(hm, wu_ref[j])
        acc = acc + _dot(a.astype(BF16), wd_ref[j])
    r = DN_ALPHA * h + (0.5 * gate) * acc
    o_ref[...] = _layer_norm_rows(r, g_ref[...], b_ref[...])


def _ffn_call(st, h, mods, layer, mi, weights, ln_g, ln_b):
    wg, wu, wd = weights
    return pl.pallas_call(
        functools.partial(_ffn_kernel, mi=mi),
        grid=st.grid,
        in_specs=[
            _tok_spec(st, D_MODEL),
            _mod_spec(layer, st),
            _const_spec((NF, D_MODEL, TF)),
            _const_spec((NF, D_MODEL, TF)),
            _const_spec((NF, TF, D_MODEL)),
            _const_spec((1, D_MODEL)),
            _const_spec((1, D_MODEL)),
        ],
        out_specs=_tok_spec(st, D_MODEL),
        out_shape=_tok_shape(st, D_MODEL),
        compiler_params=_cparams(2),
        name="ffn",
    )(h, mods, wg, wu, wd, ln_g, ln_b)


def _ffn_weights(wg, wu, wd):
    wg = wg.astype(BF16).reshape(D_MODEL, NF, TF).transpose(1, 0, 2)
    wu = wu.astype(BF16).reshape(D_MODEL, NF, TF).transpose(1, 0, 2)
    wd = wd.astype(BF16).reshape(NF, TF, D_MODEL)
    return wg, wu, wd


def _inproj_kernel(h_ref, m_ref, w_ref, *o_refs, widths):
    u = (h_ref[...] * (1.0 + m_ref[4:5, :]) + m_ref[3:4, :]).astype(BF16)
    off = 0
    for o_ref, wd in zip(o_refs, widths):
        o_ref[...] = _dot(u, w_ref[:, off:off + wd])
        off += wd


def _inproj_call(st, h, mods, layer, w, widths):
    n_in = sum(widths)
    return pl.pallas_call(
        functools.partial(_inproj_kernel, widths=widths),
        grid=st.grid,
        in_specs=[_tok_spec(st, D_MODEL), _mod_spec(layer, st), _const_spec((D_MODEL, n_in))],
        out_specs=[_tok_spec(st, wd) for wd in widths],
        out_shape=[_tok_shape(st, wd) for wd in widths],
        compiler_params=_cparams(2),
        name="mixer_in",
    )(h, mods, w)


def _hyena_feature_table(L):
    t = jnp.linspace(0.0, 1.0, L, dtype=F32)[:, None]
    bands = (HY_EMB - 1) // 2
    w = 2.0 * math.pi * jnp.arange(L, dtype=F32)[:, None] / L
    f = jnp.linspace(1e-4, bands - 1, bands, dtype=F32)[None, :]
    z = jnp.concatenate([t, jnp.cos(f * w), -jnp.sin(f * w)], axis=-1)
    z = jnp.pad(z, ((0, 0), (0, LANES - HY_EMB)))
    deltas = jnp.linspace(math.log(HY_TARGET) / HY_SLOW, math.log(HY_TARGET) / HY_FAST, HY_W, dtype=F32)
    neg = (L - jnp.arange(L)) % L
    return z, t, z[neg], t[neg], jnp.abs(deltas)[None, :]


def _hy_filter_kernel(zp_ref, tp_ref, zn_ref, tn_ref, d_ref, w1_ref, b1_ref, f1_ref, w2_ref, b2_ref, f2_ref,
                      w3_ref, g_ref, *, L):
    def hidden(z):
        h = jnp.sin(f1_ref[...] * (_dot_f32(z, w1_ref[...]) + b1_ref[...]))
        return jnp.sin(f2_ref[...] * (_dot_f32(h, w2_ref[...]) + b2_ref[...]))

    h_pos = hidden(zp_ref[...])
    h_neg = hidden(zn_ref[...])
    win_pos = jnp.exp(-tp_ref[...] * d_ref[...])
    win_neg = jnp.exp(-tn_ref[...] * d_ref[...])
    rows = lax.broadcasted_iota(jnp.int32, win_neg.shape, 0)
    for o in range(HY_ORDER):
        base = o * 2 * HY_W
        g_ref[o, L:2 * L, :] = _dot_f32(h_pos, w3_ref[:, base:base + HY_W]) * win_pos
        h_bwd = _dot_f32(h_neg, w3_ref[:, base + HY_W:base + 2 * HY_W]) * win_neg
        g_ref[o, 0:L, :] = jnp.where(rows == 0, 0.0, h_bwd)


def _hy_filter_call(L, w1, b1, f1, w2, b2, f2, w3):
    zp, tp, zn, tn, deltas = _hyena_feature_table(L)
    w1p = jnp.pad(w1, ((0, LANES - HY_EMB), (0, 0)))
    return pl.pallas_call(
        functools.partial(_hy_filter_kernel, L=L),
        out_shape=jax.ShapeDtypeStruct((HY_ORDER, 2 * L, HY_W), F32),
        compiler_params=pltpu.CompilerParams(vmem_limit_bytes=VMEM_LIMIT),
        name="hyena_filter",
    )(zp, tp, zn, tn, deltas, w1p, b1[None, :], f1[None, :], w2, b2[None, :], f2[None, :], w3)


def _block_dft_tables(s):
    n = 2 * s
    f = jnp.arange(s, dtype=jnp.int32)[:, None]

    def trig(cols):
        j = jnp.arange(cols, dtype=jnp.int32)[None, :]
        ang = ((f * j) % n).astype(F32) * (2.0 * math.pi / n)
        alt = jnp.where(j % 2 == 0, 1.0, -1.0).astype(F32)
        return j, jnp.cos(ang), jnp.sin(ang), alt

    _, cos, sin, alt = trig(s)
    fwd = jnp.concatenate([cos, jnp.where(f == 0, alt, -sin)], axis=0)
    scale = jnp.where(f == 0, 1.0 / n, 2.0 / n).astype(F32)
    inv = jnp.concatenate([(scale * cos).T, jnp.where(f == 0, alt / n, -scale * sin).T], axis=1)
    j, cos, sin, alt = trig(n)
    sgn = jnp.where(f % 2 == 0, 1.0, -1.0).astype(F32)
    taps = jnp.concatenate([sgn * cos, jnp.where(f == 0, alt, -sgn * sin)], axis=0)
    taps = jnp.where(j == 0, 0.0, taps)
    return _split_bf16(fwd), _split_bf16(inv), _split_bf16(taps)


def _hy_spectrum_kernel(th_ref, tl_ref, lo_ref, hi_ref, k_ref):
    s = lo_ref.shape[0]
    a_hi, a_lo = _split_bf16(lo_ref[...])
    b_hi, b_lo = _split_bf16(hi_ref[...])
    k_ref[...] = (_dot3(th_ref[:, :s], tl_ref[:, :s], a_hi, a_lo)
                  + _dot3(th_ref[:, s:], tl_ref[:, s:], b_hi, b_lo))


def _hy_spectrum_call(L, taps_tab, taps):
    th, tl = taps_tab
    s = th.shape[0] // 2
    nd = 2 * (L // s) - 1
    return pl.pallas_call(
        _hy_spectrum_kernel,
        grid=(HY_ORDER, nd),
        in_specs=[
            pl.BlockSpec((2 * s, 2 * s), lambda o, d: (0, 0)),
            pl.BlockSpec((2 * s, 2 * s), lambda o, d: (0, 0)),
            pl.BlockSpec((None, s, HY_W), lambda o, d: (o, d, 0)),
            pl.BlockSpec((None, s, HY_W), lambda o, d: (o, d + 1, 0)),
        ],
        out_specs=pl.BlockSpec((None, None, 2 * s, HY_W), lambda o, d: (o, d, 0, 0)),
        out_shape=jax.ShapeDtypeStruct((HY_ORDER, nd, 2 * s, HY_W), F32),
        compiler_params=_cparams(2),
        name="hyena_spectrum",
    )(th, tl, taps, taps)


def _conv3_rows(z, w, b):
    n = z.shape[0]
    rows = lax.broadcasted_iota(jnp.int32, z.shape, 0)
    prev = jnp.where(rows == 0, 0.0, pltpu.roll(z, 1, 0))
    nxt = jnp.where(rows == n - 1, 0.0, pltpu.roll(z, n - 1, 0))
    return prev * w[0:1, :] + z * w[1:2, :] + nxt * w[2:3, :] + b


def _hy_conv_kernel(u_ref, x_ref, cwu_ref, cbu_ref, cwx_ref, cbx_ref, skip_ref, k_ref, fwd_ref, inv_ref,
                    o_ref, u_sc, xf_ref, yf_ref, *, m, conv_u):
    s = inv_ref.shape[0]
    ct = o_ref.shape[-1]
    if conv_u:
        u_sc[...] = _conv3_rows(u_ref[...], cwu_ref[...], cbu_ref[...])
        src = u_sc
    else:
        src = u_ref
    o_ref[...] = _conv3_rows(x_ref[...], cwx_ref[...], cbx_ref[...])

    for jb in range(m):
        xf_ref[jb] = _dot(fwd_ref[...], src[jb * s:(jb + 1) * s, :].astype(BF16))

    for ib in range(m):
        def chunk(r, carry):
            re = pl.ds(pl.multiple_of(r * HY_MAC_ROWS, HY_MAC_ROWS), HY_MAC_ROWS)
            im = pl.ds(pl.multiple_of(s + r * HY_MAC_ROWS, HY_MAC_ROWS), HY_MAC_ROWS)
            acc_re = jnp.zeros((HY_MAC_ROWS, ct), F32)
            acc_im = jnp.zeros((HY_MAC_ROWS, ct), F32)
            for jb in range(m):
                d = ib - jb + m - 1
                x_re, x_im = xf_ref[jb, re, :], xf_ref[jb, im, :]
                k_re, k_im = k_ref[d, re, :], k_ref[d, im, :]
                acc_re = acc_re + (x_re * k_re - x_im * k_im)
                acc_im = acc_im + (x_re * k_im + x_im * k_re)
            yf_ref[ib, re, :] = acc_re
            yf_ref[ib, im, :] = acc_im
            return carry

        lax.fori_loop(0, s // HY_MAC_ROWS, chunk, 0)
        dc = jnp.zeros((1, ct), F32)
        ny = jnp.zeros((1, ct), F32)
        for jb in range(m):
            d = ib - jb + m - 1
            dc = dc + xf_ref[jb, 0:1, :] * k_ref[d, 0:1, :]
            ny = ny + xf_ref[jb, s:s + 1, :] * k_ref[d, s:s + 1, :]
        yf_ref[ib, 0:1, :] = dc
        yf_ref[ib, s:s + 1, :] = ny

    skip = skip_ref[...]
    for ib in range(m):
        conv = _dot(inv_ref[...], yf_ref[ib].astype(BF16))
        rows = slice(ib * s, (ib + 1) * s)
        o_ref[rows, :] = o_ref[rows, :] * (conv + skip * src[rows, :])


def _hy_conv_call(u, u_part, z_hy, x_part, L, order, conv_u, conv_w, conv_b, skip, spectrum, fwd_tab, inv_tab):
    s = inv_tab[0].shape[0]
    m = L // s
    nd = 2 * m - 1
    ct = HY_CT
    nct = HY_W // ct
    fwd, inv = fwd_tab[0], inv_tab[0]

    def cspec(rows, part):
        return pl.BlockSpec((rows, ct), lambda c, b: (0, part * nct + c))

    tab = lambda shape: pl.BlockSpec(shape, lambda c, b: (0, 0))
    return pl.pallas_call(
        functools.partial(_hy_conv_kernel, m=m, conv_u=conv_u),
        grid=(nct, BATCH),
        in_specs=[
            pl.BlockSpec((None, L, ct), lambda c, b: (b, 0, u_part * nct + c)),
            pl.BlockSpec((None, L, ct), lambda c, b: (b, 0, x_part * nct + c)),
            cspec(3, u_part), cspec(1, u_part), cspec(3, x_part), cspec(1, x_part),
            pl.BlockSpec((None, 1, ct), lambda c, b: (order, 0, c)),
            pl.BlockSpec((None, nd, 2 * s, ct), lambda c, b: (order, 0, 0, c)),
            tab((2 * s, s)), tab((s, 2 * s)),
        ],
        out_specs=pl.BlockSpec((None, L, ct), lambda c, b: (b, 0, c)),
        out_shape=jax.ShapeDtypeStruct((BATCH, L, HY_W), F32),
        scratch_shapes=[
            pltpu.VMEM((L, ct), F32),
            pltpu.VMEM((m, 2 * s, ct), F32),
            pltpu.VMEM((m, 2 * s, ct), F32),
        ],
        compiler_params=_cparams(2),
        name="hyena_conv",
    )(u, z_hy, conv_w, conv_b, conv_w, conv_b, skip[:, None, :], spectrum, fwd, inv)


def _hyena(z_hy, L, tables, conv_w, conv_b, f_w1, f_b1, f_freq1, f_w2, f_b2, f_freq2, f_w3, skip):
    fwd_tab, inv_tab, taps_tab = tables
    taps = _hy_filter_call(L, f_w1, f_b1, f_freq1, f_w2, f_b2, f_freq2, f_w3)
    spectrum = _hy_spectrum_call(L, taps_tab, taps)
    y1 = _hy_conv_call(z_hy, 0, z_hy, 1, L, 0, True, conv_w, conv_b, skip, spectrum, fwd_tab, inv_tab)
    return _hy_conv_call(y1, 0, z_hy, 2, L, 1, False, conv_w, conv_b, skip, spectrum, fwd_tab, inv_tab)


def _head_sum_matrix(width, head):
    i = jnp.arange(width)[:, None] // head
    j = jnp.arange(width)[None, :] // head
    return (i == j).astype(BF16)


def _group_sum(x, ones_ref):
    hi, lo = _split_bf16(x)
    return _dot(hi, ones_ref[...]) + _dot(lo, ones_ref[...])


def _rw_prep_kernel(z_ref, zp_ref, zn_ref, mu_ref, kk_ref, ka_ref, rk_ref, w0_ref, a0_ref, wup_ref, aup_ref,
                    gup_ref, ones_ref,
                    r_ref, v_ref, nkk_ref, g_ref, bonus_ref, wf_ref, kf_ref, af_ref, wb_ref, kb_ref, ab_ref):
    t = pl.program_id(1)
    z = z_ref[...]
    tm = z.shape[0]
    prev_row = jnp.where(t == 0, 0.0, zp_ref[SUBLANES - 1:SUBLANES, :])
    next_row = jnp.where(t == pl.num_programs(1) - 1, 0.0, zn_ref[0:1, :])
    rows = lax.broadcasted_iota(jnp.int32, z.shape, 0)
    prev = jnp.where(rows == 0, prev_row, pltpu.roll(z, 1, 0))
    nxt = jnp.where(rows == tm - 1, next_row, pltpu.roll(z, tm - 1, 0))
    z = z + mu_ref[...] * (0.5 * (prev + nxt) - z)

    W = RW_W
    r = z[:, 0:W]
    k = z[:, W:2 * W]
    v = z[:, 2 * W:3 * W]
    o = 3 * W
    w_lora = _dot_x3(jnp.tanh(z[:, o:o + 2 * RW_DECAY_LORA]), wup_ref[...])
    o += 2 * RW_DECAY_LORA
    a_lora = _dot_x3(z[:, o:o + 2 * RW_A_LORA], aup_ref[...])
    o += 2 * RW_A_LORA
    g = _dot_x3(jax.nn.sigmoid(z[:, o:o + RW_GATE_LORA]), gup_ref[...])

    kk = k * kk_ref[...]
    norm = jnp.sqrt(_group_sum(kk * kk, ones_ref))
    kk = kk / jnp.maximum(norm, 1e-12)

    k_sum = jnp.zeros_like(k)
    for d, (w_ref, kd_ref, ad_ref) in enumerate(((wf_ref, kf_ref, af_ref), (wb_ref, kb_ref, ab_ref))):
        y = -(w0_ref[d:d + 1, :] + w_lora[:, d * W:(d + 1) * W])
        softplus = jnp.maximum(y, 0.0) + jnp.log(1.0 + jnp.exp(-jnp.abs(y)))
        w_log = -softplus - 0.5
        a = jax.nn.sigmoid(a0_ref[d:d + 1, :] + a_lora[:, d * W:(d + 1) * W])
        kd = k * (1.0 + (a - 1.0) * ka_ref[...])
        w_ref[...] = jnp.exp(-jnp.exp(w_log))
        kd_ref[...] = kd
        ad_ref[...] = kk * a
        k_sum = k_sum + kd

    r_ref[...] = r
    v_ref[...] = v
    nkk_ref[...] = -kk
    g_ref[...] = g
    bonus_ref[...] = _group_sum(r * k_sum * rk_ref[...], ones_ref) * v


def _rw_prep_call(st, z_rw, params):
    W = RW_W
    blocks_per_tile = st.tm // SUBLANES
    n_row_blocks = st.rows // SUBLANES
    return pl.pallas_call(
        _rw_prep_kernel,
        grid=st.grid,
        in_specs=[
            _tok_spec(st, RW_IN),
            pl.BlockSpec((None, SUBLANES, RW_IN),
                         lambda g, t: (g, jnp.maximum(t * blocks_per_tile - 1, 0), 0)),
            pl.BlockSpec((None, SUBLANES, RW_IN),
                         lambda g, t: (g, jnp.minimum((t + 1) * blocks_per_tile, n_row_blocks - 1), 0)),
            _const_spec((1, RW_IN)),
            _const_spec((1, W)), _const_spec((1, W)), _const_spec((1, W)),
            _const_spec((2, W)), _const_spec((2, W)),
            _const_spec((2 * RW_DECAY_LORA, 2 * W)), _const_spec((2 * RW_A_LORA, 2 * W)),
            _const_spec((RW_GATE_LORA, W)),
            _const_spec((W, W)),
        ],
        out_specs=[_tok_spec(st, W)] * 11,
        out_shape=[_tok_shape(st, W)] * 11,
        compiler_params=_cparams(2),
        name="rwkv_prepare",
    )(z_rw, z_rw, z_rw, *params)


def _rw_prep_params(mu, k_k, k_a, r_k, w0, a0, w_up, a_up, g_up):
    W = RW_W
    zero = jnp.zeros((RW_DECAY_LORA, W), F32)
    wup = jnp.concatenate([jnp.concatenate([w_up[0], zero], 1), jnp.concatenate([zero, w_up[1]], 1)], 0)
    aup = jnp.concatenate([jnp.concatenate([a_up[0], zero], 1), jnp.concatenate([zero, a_up[1]], 1)], 0)
    return (mu[None, :], k_k[None, :], k_a[None, :], r_k.reshape(1, W), w0, a0, wup, aup, g_up,
            _head_sum_matrix(W, RW_N))


def _rw_scan_kernel(r_ref, v_ref, nkk_ref, w_ref, kd_ref, ka_ref, s0_ref, *rest, reverse, accumulate):
    if accumulate:
        oacc_ref, o_ref, sfin_ref, s_ref, sa_ref = rest
    else:
        o_ref, sfin_ref, s_ref, sa_ref = rest
    @pl.when(pl.program_id(0) == 0)
    def _():
        s_ref[...] = s0_ref[...]

    t_first = SCAN_TB - 1 if reverse else 0
    sa = jnp.zeros((RW_N, BH), F32)
    for k in range(RW_N):
        sa = sa + s_ref[k] * nkk_ref[t_first, k:k + 1, :]
    sa_ref[...] = sa

    def step(i, carry):
        t = SCAN_TB - 1 - i if reverse else i
        t_next = jnp.clip(t - 1 if reverse else t + 1, 0, SCAN_TB - 1)
        sa = sa_ref[...]
        vv = v_ref[t]
        out = jnp.zeros((RW_N, BH), F32)
        sa_next = jnp.zeros((RW_N, BH), F32)
        for k in range(RW_N):
            s_k = (s_ref[k] * w_ref[t, k:k + 1, :] + sa * ka_ref[t, k:k + 1, :]
                   + vv * kd_ref[t, k:k + 1, :])
            s_ref[k] = s_k
            out = out + s_k * r_ref[t, k:k + 1, :]
            sa_next = sa_next + s_k * nkk_ref[t_next, k:k + 1, :]
        o_ref[t] = out + oacc_ref[t] if accumulate else out
        sa_ref[...] = sa_next
        return carry

    lax.fori_loop(0, SCAN_TB, step, 0)

    @pl.when(pl.program_id(0) == pl.num_programs(0) - 1)
    def _():
        sfin_ref[...] = s_ref[...]


def _rw_scan_call(r, v, nkk, w, kd, ka, s0, reverse, o_other=None):
    n_blk = r.shape[0] // SCAN_TB
    spec = pl.BlockSpec((SCAN_TB, RW_N, BH), lambda i: ((n_blk - 1 - i) if reverse else i, 0, 0))
    state = pl.BlockSpec((RW_N, RW_N, BH), lambda i: (0, 0, 0))
    extra = [] if o_other is None else [o_other]
    return pl.pallas_call(
        functools.partial(_rw_scan_kernel, reverse=reverse, accumulate=o_other is not None),
        grid=(n_blk,),
        in_specs=[spec] * 6 + [state] + [spec] * len(extra),
        out_specs=[spec, state],
        out_shape=[jax.ShapeDtypeStruct(r.shape, F32), jax.ShapeDtypeStruct((RW_N, RW_N, BH), F32)],
        scratch_shapes=[pltpu.VMEM((RW_N, RW_N, BH), F32), pltpu.VMEM((RW_N, BH), F32)],
        compiler_params=_cparams(1),
        name="rwkv_scan",
    )(r, v, nkk, w, kd, ka, s0, *extra)


def _to_scan(x):
    n = x.shape[1]
    return x.reshape(BATCH, n, RW_H, RW_N).transpose(1, 3, 0, 2).reshape(n, RW_N, BH)


def _from_scan(x):
    n = x.shape[0]
    return x.reshape(n, RW_N, BATCH, RW_H).transpose(2, 0, 3, 1).reshape(BATCH, n, RW_W)


def _rwkv_bidir(prep_ctx, prep_lat):
    outs = {"ctx": None, "lat": None}
    shared = {name: [_to_scan(p[i]) for i in range(3)] for name, p in (("ctx", prep_ctx), ("lat", prep_lat))}
    for d, reverse in enumerate((False, True)):
        state = jnp.zeros((RW_N, RW_N, BH), F32)
        for name, p in (("ctx", prep_ctx), ("lat", prep_lat)):
            dirs = [_to_scan(p[5 + 3 * d + i]) for i in range(3)]
            outs[name], state = _rw_scan_call(*shared[name], *dirs, state, reverse, outs[name])
    return _from_scan(outs["ctx"]), _from_scan(outs["lat"])


def _even_out_kernel(h_ref, m_ref, hy_ref, osum_ref, bonus_ref, g_ref, gng_ref, gnb_ref, ones_ref,
                     why_ref, wrw_ref, lg_ref, lb_ref, o_ref):
    o = osum_ref[...]
    mu = _group_sum(o, ones_ref) * (1.0 / RW_N)
    oc = o - mu
    var = _group_sum(oc * oc, ones_ref) * (1.0 / RW_N)
    y = oc * lax.rsqrt(var + RW_GN_EPS) * gng_ref[...] + gnb_ref[...]
    rw = (y + bonus_ref[...]) * g_ref[...]
    mix = _dot(hy_ref[...].astype(BF16), why_ref[...]) + _dot(rw.astype(BF16), wrw_ref[...])
    r = DN_ALPHA * h_ref[...] + m_ref[5:6, :] * mix
    o_ref[...] = _layer_norm_rows(r, lg_ref[...], lb_ref[...])


def _even_out_call(st, h, mods, layer, hy, o_sum, bonus, g, gn_g, gn_b, w_out, ln_g, ln_b):
    W = RW_W
    return pl.pallas_call(
        _even_out_kernel,
        grid=st.grid,
        in_specs=[
            _tok_spec(st, D_MODEL), _mod_spec(layer, st),
            _tok_spec(st, HY_W), _tok_spec(st, W), _tok_spec(st, W), _tok_spec(st, W),
            _const_spec((1, W)), _const_spec((1, W)), _const_spec((W, W)),
            _const_spec((HY_W, D_MODEL)), _const_spec((W, D_MODEL)),
            _const_spec((1, D_MODEL)), _const_spec((1, D_MODEL)),
        ],
        out_specs=_tok_spec(st, D_MODEL),
        out_shape=_tok_shape(st, D_MODEL),
        compiler_params=_cparams(2),
        name="even_out",
    )(h, mods, hy, o_sum, bonus, g, gn_g[None, :], gn_b[None, :], _head_sum_matrix(W, RW_N),
      w_out[:HY_W], w_out[HY_W:], ln_g, ln_b)


GLA_GPAD = LANES
GLA_NB = 4
OD_WIDTHS = (GLA_DK, GLA_DK, GLA_DV, GLA_DV, GLA_GPAD)


def _odd_inproj_kernel(h_ref, m_ref, w_ref, gup_ref, gb_ref, q_ref, k_ref, v_ref, og_ref, gf_ref, gb_out_ref):
    u = (h_ref[...] * (1.0 + m_ref[4:5, :]) + m_ref[3:4, :]).astype(BF16)
    off = 0
    for o_ref, wd in zip((q_ref, k_ref, v_ref, og_ref), OD_WIDTHS[:4]):
        o_ref[...] = _dot(u, w_ref[:, off:off + wd])
        off += wd
    gd = _dot(u, w_ref[:, off:off + GLA_GPAD])
    logit = _dot_x3(gd, gup_ref[...]) + gb_ref[...]
    log_sig = jnp.minimum(logit, 0.0) - jnp.log(1.0 + jnp.exp(-jnp.abs(logit)))
    g = log_sig * (1.0 / GLA_NORMALIZER)
    gf_ref[...] = g[:, :GLA_DK]
    gb_out_ref[...] = g[:, GLA_DK:]


def _odd_inproj_call(st, h, mods, layer, w, gup, g_b):
    widths = OD_WIDTHS[:4] + (GLA_DK, GLA_DK)
    return pl.pallas_call(
        _odd_inproj_kernel,
        grid=st.grid,
        in_specs=[_tok_spec(st, D_MODEL), _mod_spec(layer, st), _const_spec((D_MODEL, sum(OD_WIDTHS))),
                  _const_spec((GLA_GPAD, 2 * GLA_DK)), _const_spec((1, 2 * GLA_DK))],
        out_specs=[_tok_spec(st, wd) for wd in widths],
        out_shape=[_tok_shape(st, wd) for wd in widths],
        compiler_params=_cparams(2),
        name="mixer_in_gla",
    )(h, mods, w, gup, g_b)


def _gla_chunk(q_ref, k_ref, v_ref, g_ref, o_ref, s_ref, reverse):
    C = GLA_CHUNK
    g = g_ref[...]
    ri = lax.broadcasted_iota(jnp.int32, (C, C), 0)
    ci = lax.broadcasted_iota(jnp.int32, (C, C), 1)
    causal = (ci >= ri) if reverse else (ci <= ri)
    b = _dot_f32(causal.astype(F32), g)
    mid, end = (C - 1 - C // 2, 0) if reverse else (C // 2, C - 1)
    b_mid = b[mid:mid + 1, :]
    b_end = b[end:end + 1, :]
    q = q_ref[...] * (GLA_HK ** -0.5)
    k = k_ref[...]
    q_intra = (q * jnp.exp(b - b_mid)).astype(BF16)
    k_intra = (k * jnp.exp(b_mid - b)).astype(BF16)
    q_in = (q * jnp.exp(b)).astype(BF16)
    k_out = (k * jnp.exp(b_end - b)).astype(BF16)
    d_end = jnp.exp(b_end)
    v = v_ref[...].astype(BF16)
    nt = (((1,), (1,)), ((), ()))
    tn = (((0,), (0,)), ((), ()))
    for hd in range(GLA_H):
        ks = slice(hd * GLA_HK, (hd + 1) * GLA_HK)
        vs = slice(hd * GLA_HV, (hd + 1) * GLA_HV)
        scores = lax.dot_general(q_intra[:, ks], k_intra[:, ks], nt, preferred_element_type=F32)
        scores = jnp.where(causal, scores, 0.0).astype(BF16)
        state = s_ref[hd]
        o_h = _dot(scores, v[:, vs]) + lax.dot_general(q_in[:, ks], state.astype(BF16), nt,
                                                       preferred_element_type=F32)
        o_ref[:, vs] = o_h
        s_ref[hd] = state * d_end[:, ks] + lax.dot_general(v[:, vs], k_out[:, ks], tn,
                                                           preferred_element_type=F32)


def _gla_kernel(qf_ref, kf_ref, vf_ref, gf_ref, qb_ref, kb_ref, vb_ref, gb_ref, s0f_ref, s0b_ref,
                of_ref, ob_ref, sff_ref, sfb_ref, sf_sc, sb_sc):
    @pl.when(pl.program_id(1) == 0)
    def _():
        sf_sc[...] = s0f_ref[...]
        sb_sc[...] = s0b_ref[...]

    for n in range(GLA_NB):
        _gla_chunk(qf_ref.at[n], kf_ref.at[n], vf_ref.at[n], gf_ref.at[n], of_ref.at[n], sf_sc.at[n], False)
        _gla_chunk(qb_ref.at[n], kb_ref.at[n], vb_ref.at[n], gb_ref.at[n], ob_ref.at[n], sb_sc.at[n], True)

    @pl.when(pl.program_id(1) == pl.num_programs(1) - 1)
    def _():
        sff_ref[...] = sf_sc[...]
        sfb_ref[...] = sb_sc[...]


def _gla_call(q, k, v, g_f, g_b, s0_f, s0_b):
    C = GLA_CHUNK
    n_chunk = q.shape[1] // C

    def spec(width, reverse):
        return pl.BlockSpec((GLA_NB, C, width), lambda b, i: (b, (n_chunk - 1 - i) if reverse else i, 0))

    state = pl.BlockSpec((GLA_NB, GLA_H, GLA_HV, GLA_HK), lambda b, i: (b, 0, 0, 0))
    ins = lambda reverse: [spec(GLA_DK, reverse), spec(GLA_DK, reverse), spec(GLA_DV, reverse),
                           spec(GLA_DK, reverse)]
    o_shape = jax.ShapeDtypeStruct((BATCH, q.shape[1], GLA_DV), F32)
    s_shape = jax.ShapeDtypeStruct((BATCH, GLA_H, GLA_HV, GLA_HK), F32)
    return pl.pallas_call(
        _gla_kernel,
        grid=(BATCH // GLA_NB, n_chunk),
        in_specs=ins(False) + ins(True) + [state, state],
        out_specs=[spec(GLA_DV, False), spec(GLA_DV, True), state, state],
        out_shape=[o_shape, o_shape, s_shape, s_shape],
        scratch_shapes=[pltpu.VMEM((GLA_NB, GLA_H, GLA_HV, GLA_HK), F32)] * 2,
        compiler_params=_cparams(2),
        name="gla_scan",
    )(q, k, v, g_f, q, k, v, g_b, s0_f, s0_b)


def _odd_out_kernel(h_ref, m_ref, of_ref, ob_ref, og_ref, ng_ref, w_ref, lg_ref, lb_ref, o_ref):
    o = of_ref[...] + ob_ref[...]
    og = og_ref[...]
    parts = []
    for hd in range(GLA_H):
        o_h = o[:, hd * GLA_HV:(hd + 1) * GLA_HV]
        ms = jnp.mean(o_h * o_h, axis=-1, keepdims=True)
        parts.append(o_h * lax.rsqrt(ms + GLA_EPS) * ng_ref[...])
    y = jnp.concatenate(parts, axis=-1) * _silu(og)
    mix = _dot(y.astype(BF16), w_ref[...])
    r = DN_ALPHA * h_ref[...] + m_ref[5:6, :] * mix
    o_ref[...] = _layer_norm_rows(r, lg_ref[...], lb_ref[...])


def _odd_out_call(st, h, mods, layer, o_f, o_b, og, norm_g, w_out, ln_g, ln_b):
    return pl.pallas_call(
        _odd_out_kernel,
        grid=st.grid,
        in_specs=[
            _tok_spec(st, D_MODEL), _mod_spec(layer, st),
            _tok_spec(st, GLA_DV), _tok_spec(st, GLA_DV), _tok_spec(st, GLA_DV),
            _const_spec((1, GLA_HV)), _const_spec((GLA_DV, D_MODEL)),
            _const_spec((1, D_MODEL)), _const_spec((1, D_MODEL)),
        ],
        out_specs=_tok_spec(st, D_MODEL),
        out_shape=_tok_shape(st, D_MODEL),
        compiler_params=_cparams(2),
        name="odd_out",
    )(h, mods, o_f, o_b, og, norm_g[None, :], w_out, ln_g, ln_b)


def _raster_to_columns(a):
    return a.reshape(BATCH, SEQ // GRID_W, GRID_W, a.shape[-1]).swapaxes(1, 2).reshape(a.shape)


def _columns_to_raster(a):
    return a.reshape(BATCH, GRID_W, SEQ // GRID_W, a.shape[-1]).swapaxes(1, 2).reshape(a.shape)


def kernel(x, c, ctx, c_ctx, ada_w, ada_b, ln_g, ln_b, ffn_wg, ffn_wu, ffn_wd, ev_w_in, ev_w_out, hy_conv_w, hy_conv_b, hy_f_w1, hy_f_b1, hy_f_freq1, hy_f_w2, hy_f_b2, hy_f_freq2, hy_f_w3, hy_skip, rw_mu, rw_w0, rw_w_up, rw_a0, rw_a_up, rw_g_up, rw_k_k, rw_k_a, rw_r_k, rw_gn_g, rw_gn_b, od_w_in, od_w_out, gla_g_up, gla_g_b, gla_norm_g):
    h_lat, h_ctx = x, _pair_ctx(ctx)
    s = jnp.concatenate([c, c_ctx[None, :], jnp.zeros((MOD_ROWS - BATCH - 1, D_MODEL), F32)], axis=0)
    mods = _ada_call(s, ada_w, ada_b).reshape(DEPTH, MOD_ROWS, 9, D_MODEL)
    dft_lat, dft_ctx = _block_dft_tables(min(HY_S, SEQ)), _block_dft_tables(min(HY_S, CTX_LEN))

    for l in range(DEPTH):
        last = l == DEPTH - 1
        lg = lambda i: ln_g[l, i][None, :]
        lb = lambda i: ln_b[l, i][None, :]
        w_ffn = _ffn_weights(ffn_wg[l, 0], ffn_wu[l, 0], ffn_wd[l, 0])
        h_lat = _ffn_call(LAT, h_lat, mods, l, 0, w_ffn, lg(0), lb(0))
        h_ctx = _ffn_call(CTX, h_ctx, mods, l, 0, w_ffn, lg(0), lb(0))

        if l % 2 == 0:
            e = l // 2
            w_in = ev_w_in[e].astype(BF16)
            w_out = ev_w_out[e].astype(BF16)
            hy_params = (hy_conv_w[e], hy_conv_b[e][None, :], hy_f_w1[e], hy_f_b1[e], hy_f_freq1[e], hy_f_w2[e],
                         hy_f_b2[e], hy_f_freq2[e], hy_f_w3[e], hy_skip[e])
            rw_params = _rw_prep_params(rw_mu[e], rw_k_k[e], rw_k_a[e], rw_r_k[e], rw_w0[e], rw_a0[e],
                                        rw_w_up[e], rw_a_up[e], rw_g_up[e])
            zl_hy, zl_rw = _inproj_call(LAT, h_lat, mods, l, w_in, (HY_IN, RW_IN))
            zc_hy, zc_rw = _inproj_call(CTX, h_ctx, mods, l, w_in, (HY_IN, RW_IN))
            hy_lat = _hyena(zl_hy, SEQ, dft_lat, *hy_params)
            hy_ctx = _hyena(_unpair_ctx(zc_hy), CTX_LEN, dft_ctx, *hy_params)
            prep_lat = _rw_prep_call(LAT_SEG, zl_rw, rw_params)
            prep_ctx = _rw_prep_call(CTX_SEG, _unpair_ctx(zc_rw), rw_params)
            o_ctx, o_lat = _rwkv_bidir(prep_ctx, prep_lat)
            h_lat = _even_out_call(LAT, h_lat, mods, l, hy_lat, o_lat, prep_lat[4], prep_lat[3],
                                   rw_gn_g[e], rw_gn_b[e], w_out, lg(1), lb(1))
            if not last:
                h_ctx = _even_out_call(CTX, h_ctx, mods, l, _pair_ctx(hy_ctx), _pair_ctx(o_ctx),
                                       _pair_ctx(prep_ctx[4]), _pair_ctx(prep_ctx[3]),
                                       rw_gn_g[e], rw_gn_b[e], w_out, lg(1), lb(1))
        else:
            o = l // 2
            w = od_w_in[o]
            n_qkv = 2 * GLA_DK + GLA_DV
            n_gate = 2 * GLA_GATE_LORA
            w_in = jnp.concatenate([w[:, :n_qkv], w[:, n_qkv + n_gate:], w[:, n_qkv:n_qkv + n_gate],
                                    jnp.zeros((D_MODEL, GLA_GPAD - n_gate), F32)], axis=1).astype(BF16)
            w_out = od_w_out[o].astype(BF16)
            pad = jnp.zeros((GLA_GPAD - n_gate, GLA_DK), F32)
            zero = jnp.zeros((GLA_GATE_LORA, GLA_DK), F32)
            gup = jnp.concatenate([jnp.concatenate([gla_g_up[o, 0], zero, pad], axis=0),
                                   jnp.concatenate([zero, gla_g_up[o, 1], pad], axis=0)], axis=1)
            g_bias = gla_g_b[o].reshape(1, 2 * GLA_DK)
            h_lat = _raster_to_columns(h_lat)
            ql, kl, vl, ogl, gfl, gbl = _odd_inproj_call(LAT, h_lat, mods, l, w_in, gup, g_bias)
            qc, kc, vc, ogc, gfc, gbc = [_unpair_ctx(a)
                                         for a in _odd_inproj_call(CTX, h_ctx, mods, l, w_in, gup, g_bias)]
            zero_state = jnp.zeros((BATCH, GLA_H, GLA_HV, GLA_HK), F32)
            oc_f, oc_b, s_f, s_b = _gla_call(qc, kc, vc, gfc, gbc, zero_state, zero_state)
            ol_f, ol_b, _, _ = _gla_call(ql, kl, vl, gfl, gbl, s_f, s_b)
            h_lat = _odd_out_call(LAT, h_lat, mods, l, ol_f, ol_b, ogl, gla_norm_g[o], w_out, lg(1), lb(1))
            h_lat = _columns_to_raster(h_lat)
            if not last:
                h_ctx = _odd_out_call(CTX, h_ctx, mods, l, _pair_ctx(oc_f), _pair_ctx(oc_b),
                                      _pair_ctx(ogc), gla_norm_g[o], w_out, lg(1), lb(1))

        w_ffn = _ffn_weights(ffn_wg[l, 1], ffn_wu[l, 1], ffn_wd[l, 1])
        h_lat = _ffn_call(LAT, h_lat, mods, l, 6, w_ffn, lg(2), lb(2))
        if not last:
            h_ctx = _ffn_call(CTX, h_ctx, mods, l, 6, w_ffn, lg(2), lb(2))
    return h_lat
```

```python
import functools
import math
from typing import NamedTuple

import jax
import jax.numpy as jnp
from jax import lax
from jax.experimental import pallas as pl
from jax.experimental.pallas import tpu as pltpu

F32 = jnp.float32
BF16 = jnp.bfloat16
HIGHEST = lax.Precision.HIGHEST

D_MODEL = 1024
BATCH = 16
SEQ = 2048
DEPTH = 4
GRID_W = 64
CTX_LEN = 256
DN_ALPHA = (2 * DEPTH) ** 0.25
LN_EPS = 1e-6
D_FF = 2816

HY_W = D_MODEL // 2
HY_ORDER = 2
HY_IN = (HY_ORDER + 1) * HY_W
HY_EMB = 33
HY_FO = 64
HY_TARGET = 1e-2
HY_FAST = 0.3
HY_SLOW = 1.5

RW_W = D_MODEL - HY_W
RW_N = 64
RW_H = RW_W // RW_N
RW_DECAY_LORA = 64
RW_A_LORA = 64
RW_GATE_LORA = 128
RW_GN_EPS = 64e-5
RW_IN = 3 * RW_W + 2 * RW_DECAY_LORA + 2 * RW_A_LORA + RW_GATE_LORA
EV_IN = HY_IN + RW_IN

GLA_H = 4
GLA_DK = D_MODEL // 2
GLA_DV = D_MODEL
GLA_HK = GLA_DK // GLA_H
GLA_HV = GLA_DV // GLA_H
GLA_GATE_LORA = 16
GLA_NORMALIZER = 16.0
GLA_CHUNK = 64
GLA_EPS = 1e-5

LANES = 128
SUBLANES = 8

MOD_ROWS = 24
TF = 256
NF = D_FF // TF
SCAN_TB = 32
BH = BATCH * RW_H
HY_S = 512
HY_CT = 256
HY_MAC_ROWS = 32
VMEM_LIMIT = 56 * 1024 * 1024


class _Stream(NamedTuple):
    groups: int
    rows: int
    tm: int
    ctx: bool

    @property
    def grid(self):
        return (self.groups, self.rows // self.tm)


LAT = _Stream(BATCH, SEQ, 512, False)
CTX = _Stream(BATCH // 2, 2 * CTX_LEN, 512, True)
LAT_SEG = _Stream(BATCH, SEQ, 256, False)
CTX_SEG = _Stream(BATCH, CTX_LEN, 256, True)


def _pair_ctx(a):
    return a.reshape(BATCH // 2, 2 * CTX_LEN, a.shape[-1])


def _unpair_ctx(a):
    return a.reshape(BATCH, CTX_LEN, a.shape[-1])


def _cparams(n_axes):
    return pltpu.CompilerParams(dimension_semantics=("arbitrary",) * n_axes,
                                vmem_limit_bytes=VMEM_LIMIT)


def _layer_norm_rows(r, g, b):
    mu = jnp.mean(r, axis=-1, keepdims=True)
    xc = r - mu
    var = jnp.mean(xc * xc, axis=-1, keepdims=True)
    return xc * lax.rsqrt(var + LN_EPS) * g + b


def _silu(x):
    return x * jax.nn.sigmoid(x)


def _split_bf16(x):
    hi = x.astype(BF16)
    lo = (x - hi.astype(F32)).astype(BF16)
    return hi, lo


def _dot(a, b):
    return jnp.dot(a, b, preferred_element_type=F32)


def _dot_f32(a, b):
    return jnp.dot(a, b, preferred_element_type=F32, precision=HIGHEST)


def _dot3(a_hi, a_lo, b_hi, b_lo):
    return _dot(a_hi, b_hi) + (_dot(a_lo, b_hi) + _dot(a_hi, b_lo))


def _dot_x3(a, b):
    return _dot3(*_split_bf16(a), *_split_bf16(b))


def _mod_spec(layer, st):
    def index(g, t):
        return (layer, BATCH if st.ctx else g, 0, 0)
    return pl.BlockSpec((None, None, 9, D_MODEL), index)


def _tok_spec(st, width, col=0):
    return pl.BlockSpec((None, st.tm, width), lambda g, t: (g, t, col))


def _const_spec(shape):
    nd = len(shape)
    return pl.BlockSpec(shape, lambda g, t: (0,) * nd, pipeline_mode=pl.Buffered(1))


def _tok_shape(st, width):
    return jax.ShapeDtypeStruct((st.groups, st.rows, width), F32)


def _ada_kernel(s_ref, w_ref, b_ref, o_ref):
    s = _silu(s_ref[...])
    o_ref[...] = _dot_f32(s, w_ref[...]) + b_ref[...]


def _ada_call(s, ada_w, ada_b):
    tn = 2304
    n_blk = 9 * D_MODEL // tn
    return pl.pallas_call(
        _ada_kernel,
        grid=(DEPTH, n_blk),
        in_specs=[
            pl.BlockSpec((MOD_ROWS, D_MODEL), lambda l, j: (0, 0)),
            pl.BlockSpec((None, D_MODEL, tn), lambda l, j: (l, 0, j)),
            pl.BlockSpec((None, 1, tn), lambda l, j: (l, 0, j)),
        ],
        out_specs=pl.BlockSpec((None, MOD_ROWS, tn), lambda l, j: (l, 0, j)),
        out_shape=jax.ShapeDtypeStruct((DEPTH, MOD_ROWS, 9 * D_MODEL), F32),
        compiler_params=_cparams(2),
        name="ada_mod",
    )(s, ada_w, ada_b.reshape(DEPTH, 1, 9 * D_MODEL))


def _ffn_rows(h, m_ref, mi, wg_ref, wu_ref, wd_ref, g_ref, b_ref):
    shift = m_ref[mi:mi + 1, :]
    scale = m_ref[mi + 1:mi + 2, :]
    gate = m_ref[mi + 2:mi + 3, :]
    hm = (h * (1.0 + scale) + shift).astype(BF16)
    acc = jnp.zeros(h.shape, F32)
    for j in range(NF):
        a = _silu(_dot(hm, wg_ref[j])) * _dot(hm, wu_ref[j])
        acc = acc + _dot(a.astype(BF16), wd_ref[j])
    r = DN_ALPHA * h + (0.5 * gate) * acc
    return _layer_norm_rows(r, g_ref[...], b_ref[...])


def _ffn_specs():
    return [_const_spec((NF, D_MODEL, TF)), _const_spec((NF, D_MODEL, TF)), _const_spec((NF, TF, D_MODEL)),
            _const_spec((1, D_MODEL)), _const_spec((1, D_MODEL))]


def _ffn_kernel(h_ref, m_ref, wg_ref, wu_ref, wd_ref, g_ref, b_ref, o_ref):
    o_ref[...] = _ffn_rows(h_ref[...], m_ref, 0, wg_ref, wu_ref, wd_ref, g_ref, b_ref)


def _ffn_call(st, h, mods, layer, weights, ln_g, ln_b):
    return pl.pallas_call(
        _ffn_kernel,
        grid=st.grid,
        in_specs=[_tok_spec(st, D_MODEL), _mod_spec(layer, st)] + _ffn_specs(),
        out_specs=_tok_spec(st, D_MODEL),
        out_shape=_tok_shape(st, D_MODEL),
        compiler_params=_cparams(2),
        name="ffn",
    )(h, mods, *weights, ln_g, ln_b)


def _ffn_weights(wg, wu, wd):
    wg = wg.astype(BF16).reshape(D_MODEL, NF, TF).transpose(1, 0, 2)
    wu = wu.astype(BF16).reshape(D_MODEL, NF, TF).transpose(1, 0, 2)
    wd = wd.astype(BF16).reshape(NF, TF, D_MODEL)
    return wg, wu, wd


def _inproj_kernel(h_ref, m_ref, w_ref, *o_refs, widths):
    u = (h_ref[...] * (1.0 + m_ref[4:5, :]) + m_ref[3:4, :]).astype(BF16)
    off = 0
    for o_ref, wd in zip(o_refs, widths):
        o_ref[...] = _dot(u, w_ref[:, off:off + wd])
        off += wd


def _inproj_call(st, h, mods, layer, w, widths):
    n_in = sum(widths)
    return pl.pallas_call(
        functools.partial(_inproj_kernel, widths=widths),
        grid=st.grid,
        in_specs=[_tok_spec(st, D_MODEL), _mod_spec(layer, st), _const_spec((D_MODEL, n_in))],
        out_specs=[_tok_spec(st, wd) for wd in widths],
        out_shape=[_tok_shape(st, wd) for wd in widths],
        compiler_params=_cparams(2),
        name="mixer_in",
    )(h, mods, w)


def _hyena_feature_table(L):
    t = jnp.linspace(0.0, 1.0, L, dtype=F32)[:, None]
    bands = (HY_EMB - 1) // 2
    w = 2.0 * math.pi * jnp.arange(L, dtype=F32)[:, None] / L
    f = jnp.linspace(1e-4, bands - 1, bands, dtype=F32)[None, :]
    z = jnp.concatenate([t, jnp.cos(f * w), -jnp.sin(f * w)], axis=-1)
    z = jnp.pad(z, ((0, 0), (0, LANES - HY_EMB)))
    deltas = jnp.linspace(math.log(HY_TARGET) / HY_SLOW, math.log(HY_TARGET) / HY_FAST, HY_W, dtype=F32)
    neg = (L - jnp.arange(L)) % L
    return z, t, z[neg], t[neg], jnp.abs(deltas)[None, :]


def _hy_filter_kernel(zp_ref, tp_ref, zn_ref, tn_ref, d_ref, w1_ref, b1_ref, f1_ref, w2_ref, b2_ref, f2_ref,
                      w3_ref, g_ref, *, L):
    def hidden(z):
        h = jnp.sin(f1_ref[...] * (_dot_f32(z, w1_ref[...]) + b1_ref[...]))
        return jnp.sin(f2_ref[...] * (_dot_f32(h, w2_ref[...]) + b2_ref[...]))

    h_pos = hidden(zp_ref[...])
    h_neg = hidden(zn_ref[...])
    win_pos = jnp.exp(-tp_ref[...] * d_ref[...])
    win_neg = jnp.exp(-tn_ref[...] * d_ref[...])
    rows = lax.broadcasted_iota(jnp.int32, win_neg.shape, 0)
    for o in range(HY_ORDER):
        base = o * 2 * HY_W
        g_ref[o, L:2 * L, :] = _dot_f32(h_pos, w3_ref[:, base:base + HY_W]) * win_pos
        h_bwd = _dot_f32(h_neg, w3_ref[:, base + HY_W:base + 2 * HY_W]) * win_neg
        g_ref[o, 0:L, :] = jnp.where(rows == 0, 0.0, h_bwd)


def _hy_filter_call(L, w1, b1, f1, w2, b2, f2, w3):
    zp, tp, zn, tn, deltas = _hyena_feature_table(L)
    w1p = jnp.pad(w1, ((0, LANES - HY_EMB), (0, 0)))
    return pl.pallas_call(
        functools.partial(_hy_filter_kernel, L=L),
        out_shape=jax.ShapeDtypeStruct((HY_ORDER, 2 * L, HY_W), F32),
        compiler_params=pltpu.CompilerParams(vmem_limit_bytes=VMEM_LIMIT),
        name="hyena_filter",
    )(zp, tp, zn, tn, deltas, w1p, b1[None, :], f1[None, :], w2, b2[None, :], f2[None, :], w3)


def _block_dft_tables(s):
    n = 2 * s
    f = jnp.arange(s, dtype=jnp.int32)[:, None]

    def trig(cols):
        j = jnp.arange(cols, dtype=jnp.int32)[None, :]
        ang = ((f * j) % n).astype(F32) * (2.0 * math.pi / n)
        alt = jnp.where(j % 2 == 0, 1.0, -1.0).astype(F32)
        return j, jnp.cos(ang), jnp.sin(ang), alt

    _, cos, sin, alt = trig(s)
    fwd = jnp.concatenate([cos, jnp.where(f == 0, alt, -sin)], axis=0)
    scale = jnp.where(f == 0, 1.0 / n, 2.0 / n).astype(F32)
    inv = jnp.concatenate([(scale * cos).T, jnp.where(f == 0, alt / n, -scale * sin).T], axis=1)
    j, cos, sin, alt = trig(n)
    sgn = jnp.where(f % 2 == 0, 1.0, -1.0).astype(F32)
    taps = jnp.concatenate([sgn * cos, jnp.where(f == 0, alt, -sgn * sin)], axis=0)
    taps = jnp.where(j == 0, 0.0, taps)
    return _split_bf16(fwd), _split_bf16(inv), _split_bf16(taps)


def _hy_spectrum_kernel(th_ref, tl_ref, lo_ref, hi_ref, k_ref):
    s = lo_ref.shape[0]
    a_hi, a_lo = _split_bf16(lo_ref[...])
    b_hi, b_lo = _split_bf16(hi_ref[...])
    k_ref[...] = (_dot3(th_ref[:, :s], tl_ref[:, :s], a_hi, a_lo)
                  + _dot3(th_ref[:, s:], tl_ref[:, s:], b_hi, b_lo))


def _hy_spectrum_call(L, taps_tab, taps):
    th, tl = taps_tab
    s = th.shape[0] // 2
    nd = 2 * (L // s) - 1
    return pl.pallas_call(
        _hy_spectrum_kernel,
        grid=(HY_ORDER, nd),
        in_specs=[
            pl.BlockSpec((2 * s, 2 * s), lambda o, d: (0, 0)),
            pl.BlockSpec((2 * s, 2 * s), lambda o, d: (0, 0)),
            pl.BlockSpec((None, s, HY_W), lambda o, d: (o, d, 0)),
            pl.BlockSpec((None, s, HY_W), lambda o, d: (o, d + 1, 0)),
        ],
        out_specs=pl.BlockSpec((None, None, 2 * s, HY_W), lambda o, d: (o, d, 0, 0)),
        out_shape=jax.ShapeDtypeStruct((HY_ORDER, nd, 2 * s, HY_W), F32),
        compiler_params=_cparams(2),
        name="hyena_spectrum",
    )(th, tl, taps, taps)


def _conv3_rows(z, w, b):
    n = z.shape[0]
    rows = lax.broadcasted_iota(jnp.int32, z.shape, 0)
    prev = jnp.where(rows == 0, 0.0, pltpu.roll(z, 1, 0))
    nxt = jnp.where(rows == n - 1, 0.0, pltpu.roll(z, n - 1, 0))
    return prev * w[0:1, :] + z * w[1:2, :] + nxt * w[2:3, :] + b


def _hy_conv_kernel(u_ref, x_ref, cwu_ref, cbu_ref, cwx_ref, cbx_ref, skip_ref, k_ref, fwd_ref, inv_ref,
                    o_ref, u_sc, xf_ref, yf_ref, *, m, conv_u):
    s = inv_ref.shape[0]
    ct = o_ref.shape[-1]
    if conv_u:
        u_sc[...] = _conv3_rows(u_ref[...], cwu_ref[...], cbu_ref[...])
        src = u_sc
    else:
        src = u_ref
    o_ref[...] = _conv3_rows(x_ref[...], cwx_ref[...], cbx_ref[...])

    for jb in range(m):
        xf_ref[jb] = _dot(fwd_ref[...], src[jb * s:(jb + 1) * s, :].astype(BF16))

    for ib in range(m):
        def chunk(r, carry):
            re = pl.ds(pl.multiple_of(r * HY_MAC_ROWS, HY_MAC_ROWS), HY_MAC_ROWS)
            im = pl.ds(pl.multiple_of(s + r * HY_MAC_ROWS, HY_MAC_ROWS), HY_MAC_ROWS)
            acc_re = jnp.zeros((HY_MAC_ROWS, ct), F32)
            acc_im = jnp.zeros((HY_MAC_ROWS, ct), F32)
            for jb in range(m):
                d = ib - jb + m - 1
                x_re, x_im = xf_ref[jb, re, :], xf_ref[jb, im, :]
                k_re, k_im = k_ref[d, re, :], k_ref[d, im, :]
                acc_re = acc_re + (x_re * k_re - x_im * k_im)
                acc_im = acc_im + (x_re * k_im + x_im * k_re)
            yf_ref[ib, re, :] = acc_re
            yf_ref[ib, im, :] = acc_im
            return carry

        lax.fori_loop(0, s // HY_MAC_ROWS, chunk, 0)
        dc = jnp.zeros((1, ct), F32)
        ny = jnp.zeros((1, ct), F32)
        for jb in range(m):
            d = ib - jb + m - 1
            dc = dc + xf_ref[jb, 0:1, :] * k_ref[d, 0:1, :]
            ny = ny + xf_ref[jb, s:s + 1, :] * k_ref[d, s:s + 1, :]
        yf_ref[ib, 0:1, :] = dc
        yf_ref[ib, s:s + 1, :] = ny

    skip = skip_ref[...]
    for ib in range(m):
        conv = _dot(inv_ref[...], yf_ref[ib].astype(BF16))
        rows = slice(ib * s, (ib + 1) * s)
        o_ref[rows, :] = o_ref[rows, :] * (conv + skip * src[rows, :])


def _hy_conv_call(u, u_part, z_hy, x_part, L, order, conv_u, conv_w, conv_b, skip, spectrum, fwd_tab, inv_tab):
    s = inv_tab[0].shape[0]
    m = L // s
    nd = 2 * m - 1
    ct = HY_CT
    nct = HY_W // ct
    fwd, inv = fwd_tab[0], inv_tab[0]

    def cspec(rows, part):
        return pl.BlockSpec((rows, ct), lambda c, b: (0, part * nct + c))

    tab = lambda shape: pl.BlockSpec(shape, lambda c, b: (0, 0))
    return pl.pallas_call(
        functools.partial(_hy_conv_kernel, m=m, conv_u=conv_u),
        grid=(nct, BATCH),
        in_specs=[
            pl.BlockSpec((None, L, ct), lambda c, b: (b, 0, u_part * nct + c)),
            pl.BlockSpec((None, L, ct), lambda c, b: (b, 0, x_part * nct + c)),
            cspec(3, u_part), cspec(1, u_part), cspec(3, x_part), cspec(1, x_part),
            pl.BlockSpec((None, 1, ct), lambda c, b: (order, 0, c)),
            pl.BlockSpec((None, nd, 2 * s, ct), lambda c, b: (order, 0, 0, c)),
            tab((2 * s, s)), tab((s, 2 * s)),
        ],
        out_specs=pl.BlockSpec((None, L, ct), lambda c, b: (b, 0, c)),
        out_shape=jax.ShapeDtypeStruct((BATCH, L, HY_W), F32),
        scratch_shapes=[
            pltpu.VMEM((L, ct), F32),
            pltpu.VMEM((m, 2 * s, ct), F32),
            pltpu.VMEM((m, 2 * s, ct), F32),
        ],
        compiler_params=_cparams(2),
        name="hyena_conv",
    )(u, z_hy, conv_w, conv_b, conv_w, conv_b, skip[:, None, :], spectrum, fwd, inv)


def _hyena(z_hy, L, tables, conv_w, conv_b, f_w1, f_b1, f_freq1, f_w2, f_b2, f_freq2, f_w3, skip):
    fwd_tab, inv_tab, taps_tab = tables
    taps = _hy_filter_call(L, f_w1, f_b1, f_freq1, f_w2, f_b2, f_freq2, f_w3)
    spectrum = _hy_spectrum_call(L, taps_tab, taps)
    y1 = _hy_conv_call(z_hy, 0, z_hy, 1, L, 0, True, conv_w, conv_b, skip, spectrum, fwd_tab, inv_tab)
    return _hy_conv_call(y1, 0, z_hy, 2, L, 1, False, conv_w, conv_b, skip, spectrum, fwd_tab, inv_tab)


def _head_sum_matrix(width, head):
    i = jnp.arange(width)[:, None] // head
    j = jnp.arange(width)[None, :] // head
    return (i == j).astype(BF16)


def _group_sum(x, ones_ref):
    hi, lo = _split_bf16(x)
    return _dot(hi, ones_ref[...]) + _dot(lo, ones_ref[...])


def _rw_prep_kernel(z_ref, zp_ref, zn_ref, mu_ref, kk_ref, ka_ref, rk_ref, w0_ref, a0_ref, wup_ref, aup_ref,
                    gup_ref, ones_ref,
                    r_ref, v_ref, nkk_ref, g_ref, bonus_ref, wf_ref, kf_ref, af_ref, wb_ref, kb_ref, ab_ref):
    t = pl.program_id(1)
    z = z_ref[...]
    tm = z.shape[0]
    prev_row = jnp.where(t == 0, 0.0, zp_ref[SUBLANES - 1:SUBLANES, :])
    next_row = jnp.where(t == pl.num_programs(1) - 1, 0.0, zn_ref[0:1, :])
    rows = lax.broadcasted_iota(jnp.int32, z.shape, 0)
    prev = jnp.where(rows == 0, prev_row, pltpu.roll(z, 1, 0))
    nxt = jnp.where(rows == tm - 1, next_row, pltpu.roll(z, tm - 1, 0))
    z = z + mu_ref[...] * (0.5 * (prev + nxt) - z)

    W = RW_W
    r = z[:, 0:W]
    k = z[:, W:2 * W]
    v = z[:, 2 * W:3 * W]
    o = 3 * W
    w_lora = _dot_x3(jnp.tanh(z[:, o:o + 2 * RW_DECAY_LORA]), wup_ref[...])
    o += 2 * RW_DECAY_LORA
    a_lora = _dot_x3(z[:, o:o + 2 * RW_A_LORA], aup_ref[...])
    o += 2 * RW_A_LORA
    g = _dot_x3(jax.nn.sigmoid(z[:, o:o + RW_GATE_LORA]), gup_ref[...])

    kk = k * kk_ref[...]
    norm = jnp.sqrt(_group_sum(kk * kk, ones_ref))
    kk = kk / jnp.maximum(norm, 1e-12)

    k_sum = jnp.zeros_like(k)
    for d, (w_ref, kd_ref, ad_ref) in enumerate(((wf_ref, kf_ref, af_ref), (wb_ref, kb_ref, ab_ref))):
        y = -(w0_ref[d:d + 1, :] + w_lora[:, d * W:(d + 1) * W])
        softplus = jnp.maximum(y, 0.0) + jnp.log(1.0 + jnp.exp(-jnp.abs(y)))
        w_log = -softplus - 0.5
        a = jax.nn.sigmoid(a0_ref[d:d + 1, :] + a_lora[:, d * W:(d + 1) * W])
        kd = k * (1.0 + (a - 1.0) * ka_ref[...])
        w_ref[...] = jnp.exp(-jnp.exp(w_log))
        kd_ref[...] = kd
        ad_ref[...] = kk * a
        k_sum = k_sum + kd

    r_ref[...] = r
    v_ref[...] = v
    nkk_ref[...] = -kk
    g_ref[...] = g
    bonus_ref[...] = _group_sum(r * k_sum * rk_ref[...], ones_ref) * v


def _rw_prep_call(st, z_rw, params):
    W = RW_W
    blocks_per_tile = st.tm // SUBLANES
    n_row_blocks = st.rows // SUBLANES
    return pl.pallas_call(
        _rw_prep_kernel,
        grid=st.grid,
        in_specs=[
            _tok_spec(st, RW_IN),
            pl.BlockSpec((None, SUBLANES, RW_IN),
                         lambda g, t: (g, jnp.maximum(t * blocks_per_tile - 1, 0), 0)),
            pl.BlockSpec((None, SUBLANES, RW_IN),
                         lambda g, t: (g, jnp.minimum((t + 1) * blocks_per_tile, n_row_blocks - 1), 0)),
            _const_spec((1, RW_IN)),
            _const_spec((1, W)), _const_spec((1, W)), _const_spec((1, W)),
            _const_spec((2, W)), _const_spec((2, W)),
            _const_spec((2 * RW_DECAY_LORA, 2 * W)), _const_spec((2 * RW_A_LORA, 2 * W)),
            _const_spec((RW_GATE_LORA, W)),
            _const_spec((W, W)),
        ],
        out_specs=[_tok_spec(st, W)] * 11,
        out_shape=[_tok_shape(st, W)] * 11,
        compiler_params=_cparams(2),
        name="rwkv_prepare",
    )(z_rw, z_rw, z_rw, *params)


def _rw_prep_params(mu, k_k, k_a, r_k, w0, a0, w_up, a_up, g_up):
    W = RW_W
    zero = jnp.zeros((RW_DECAY_LORA, W), F32)
    wup = jnp.concatenate([jnp.concatenate([w_up[0], zero], 1), jnp.concatenate([zero, w_up[1]], 1)], 0)
    aup = jnp.concatenate([jnp.concatenate([a_up[0], zero], 1), jnp.concatenate([zero, a_up[1]], 1)], 0)
    return (mu[None, :], k_k[None, :], k_a[None, :], r_k.reshape(1, W), w0, a0, wup, aup, g_up,
            _head_sum_matrix(W, RW_N))


def _rw_scan_kernel(r_ref, v_ref, nkk_ref, w_ref, kd_ref, ka_ref, s0_ref, *rest, reverse, accumulate):
    if accumulate:
        oacc_ref, o_ref, sfin_ref, s_ref, sa_ref = rest
    else:
        o_ref, sfin_ref, s_ref, sa_ref = rest
    @pl.when(pl.program_id(0) == 0)
    def _():
        s_ref[...] = s0_ref[...]

    t_first = SCAN_TB - 1 if reverse else 0
    sa = jnp.zeros((RW_N, BH), F32)
    for k in range(RW_N):
        sa = sa + s_ref[k] * nkk_ref[t_first, k:k + 1, :]
    sa_ref[...] = sa

    def step(i, carry):
        t = SCAN_TB - 1 - i if reverse else i
        t_next = jnp.clip(t - 1 if reverse else t + 1, 0, SCAN_TB - 1)
        sa = sa_ref[...]
        vv = v_ref[t]
        out = jnp.zeros((RW_N, BH), F32)
        sa_next = jnp.zeros((RW_N, BH), F32)
        for k in range(RW_N):
            s_k = (s_ref[k] * w_ref[t, k:k + 1, :] + sa * ka_ref[t, k:k + 1, :]
                   + vv * kd_ref[t, k:k + 1, :])
            s_ref[k] = s_k
            out = out + s_k * r_ref[t, k:k + 1, :]
            sa_next = sa_next + s_k * nkk_ref[t_next, k:k + 1, :]
        o_ref[t] = out + oacc_ref[t] if accumulate else out
        sa_ref[...] = sa_next
        return carry

    lax.fori_loop(0, SCAN_TB, step, 0)

    @pl.when(pl.program_id(0) == pl.num_programs(0) - 1)
    def _():
        sfin_ref[...] = s_ref[...]


def _rw_scan_call(r, v, nkk, w, kd, ka, s0, reverse, o_other=None):
    n_blk = r.shape[0] // SCAN_TB
    spec = pl.BlockSpec((SCAN_TB, RW_N, BH), lambda i: ((n_blk - 1 - i) if reverse else i, 0, 0))
    state = pl.BlockSpec((RW_N, RW_N, BH), lambda i: (0, 0, 0))
    extra = [] if o_other is None else [o_other]
    return pl.pallas_call(
        functools.partial(_rw_scan_kernel, reverse=reverse, accumulate=o_other is not None),
        grid=(n_blk,),
        in_specs=[spec] * 6 + [state] + [spec] * len(extra),
        out_specs=[spec, state],
        out_shape=[jax.ShapeDtypeStruct(r.shape, F32), jax.ShapeDtypeStruct((RW_N, RW_N, BH), F32)],
        scratch_shapes=[pltpu.VMEM((RW_N, RW_N, BH), F32), pltpu.VMEM((RW_N, BH), F32)],
        compiler_params=_cparams(1),
        name="rwkv_scan",
    )(r, v, nkk, w, kd, ka, s0, *extra)


def _to_scan(x):
    n = x.shape[1]
    return x.reshape(BATCH, n, RW_H, RW_N).transpose(1, 3, 0, 2).reshape(n, RW_N, BH)


def _from_scan(x):
    n = x.shape[0]
    return x.reshape(n, RW_N, BATCH, RW_H).transpose(2, 0, 3, 1).reshape(BATCH, n, RW_W)


def _rwkv_bidir(prep_ctx, prep_lat):
    outs = {"ctx": None, "lat": None}
    shared = {name: [_to_scan(p[i]) for i in range(3)] for name, p in (("ctx", prep_ctx), ("lat", prep_lat))}
    for d, reverse in enumerate((False, True)):
        state = jnp.zeros((RW_N, RW_N, BH), F32)
        for name, p in (("ctx", prep_ctx), ("lat", prep_lat)):
            dirs = [_to_scan(p[5 + 3 * d + i]) for i in range(3)]
            outs[name], state = _rw_scan_call(*shared[name], *dirs, state, reverse, outs[name])
    return _from_scan(outs["ctx"]), _from_scan(outs["lat"])


def _even_out_kernel(h_ref, m_ref, hy_ref, osum_ref, bonus_ref, g_ref, gng_ref, gnb_ref, ones_ref,
                     why_ref, wrw_ref, lg_ref, lb_ref, wg_ref, wu_ref, wd_ref, lg2_ref, lb2_ref, o_ref):
    o = osum_ref[...]
    mu = _group_sum(o, ones_ref) * (1.0 / RW_N)
    oc = o - mu
    var = _group_sum(oc * oc, ones_ref) * (1.0 / RW_N)
    y = oc * lax.rsqrt(var + RW_GN_EPS) * gng_ref[...] + gnb_ref[...]
    rw = (y + bonus_ref[...]) * g_ref[...]
    mix = _dot(hy_ref[...].astype(BF16), why_ref[...]) + _dot(rw.astype(BF16), wrw_ref[...])
    r = DN_ALPHA * h_ref[...] + m_ref[5:6, :] * mix
    h_mid = _layer_norm_rows(r, lg_ref[...], lb_ref[...])
    o_ref[...] = _ffn_rows(h_mid, m_ref, 6, wg_ref, wu_ref, wd_ref, lg2_ref, lb2_ref)


def _even_out_call(st, h, mods, layer, hy, o_sum, bonus, g, gn_g, gn_b, w_out, ln_g, ln_b, ffn_weights,
                   ln_g2, ln_b2):
    W = RW_W
    return pl.pallas_call(
        _even_out_kernel,
        grid=st.grid,
        in_specs=[
            _tok_spec(st, D_MODEL), _mod_spec(layer, st),
            _tok_spec(st, HY_W), _tok_spec(st, W), _tok_spec(st, W), _tok_spec(st, W),
            _const_spec((1, W)), _const_spec((1, W)), _const_spec((W, W)),
            _const_spec((HY_W, D_MODEL)), _const_spec((W, D_MODEL)),
            _const_spec((1, D_MODEL)), _const_spec((1, D_MODEL)),
        ] + _ffn_specs(),
        out_specs=_tok_spec(st, D_MODEL),
        out_shape=_tok_shape(st, D_MODEL),
        compiler_params=_cparams(2),
        name="even_out_ffn",
    )(h, mods, hy, o_sum, bonus, g, gn_g[None, :], gn_b[None, :], _head_sum_matrix(W, RW_N),
      w_out[:HY_W], w_out[HY_W:], ln_g, ln_b, *ffn_weights, ln_g2, ln_b2)


GLA_GPAD = LANES
GLA_NB = 4
OD_WIDTHS = (GLA_DK, GLA_DK, GLA_DV, GLA_DV, GLA_GPAD)


def _odd_inproj_kernel(h_ref, m_ref, w_ref, gup_ref, gb_ref, q_ref, k_ref, v_ref, og_ref, gf_ref, gb_out_ref):
    u = (h_ref[...] * (1.0 + m_ref[4:5, :]) + m_ref[3:4, :]).astype(BF16)
    gd = _dot(u, w_ref[:, sum(OD_WIDTHS[:4]):])
    logit = _dot_x3(gd, gup_ref[...]) + gb_ref[...]
    log_sig = jnp.minimum(logit, 0.0) - jnp.log(1.0 + jnp.exp(-jnp.abs(logit)))
    g = log_sig * (1.0 / GLA_NORMALIZER)
    gf_ref[...] = g[:, :GLA_DK]
    gb_out_ref[...] = g[:, GLA_DK:]
    off = 0
    for o_ref, wd in zip((q_ref, k_ref, v_ref, og_ref), OD_WIDTHS[:4]):
        o_ref[...] = _dot(u, w_ref[:, off:off + wd])
        off += wd


def _odd_inproj_call(st, h, mods, layer, w, gup, g_b):
    widths = OD_WIDTHS[:4] + (GLA_DK, GLA_DK)
    return pl.pallas_call(
        _odd_inproj_kernel,
        grid=st.grid,
        in_specs=[_tok_spec(st, D_MODEL), _mod_spec(layer, st), _const_spec((D_MODEL, sum(OD_WIDTHS))),
                  _const_spec((GLA_GPAD, 2 * GLA_DK)), _const_spec((1, 2 * GLA_DK))],
        out_specs=[_tok_spec(st, wd) for wd in widths],
        out_shape=[_tok_shape(st, wd) for wd in widths],
        compiler_params=_cparams(2),
        name="mixer_in_gla",
    )(h, mods, w, gup, g_b)


def _gla_chunk(q_ref, k_ref, v_ref, g_ref, o_ref, s_ref, reverse):
    C = GLA_CHUNK
    g = g_ref[...]
    ri = lax.broadcasted_iota(jnp.int32, (C, C), 0)
    ci = lax.broadcasted_iota(jnp.int32, (C, C), 1)
    causal = (ci >= ri) if reverse else (ci <= ri)
    b = _dot_f32(causal.astype(F32), g)
    mid, end = (C - 1 - C // 2, 0) if reverse else (C // 2, C - 1)
    b_mid = b[mid:mid + 1, :]
    b_end = b[end:end + 1, :]
    q = q_ref[...] * (GLA_HK ** -0.5)
    k = k_ref[...]
    q_intra = (q * jnp.exp(b - b_mid)).astype(BF16)
    k_intra = (k * jnp.exp(b_mid - b)).astype(BF16)
    q_in = (q * jnp.exp(b)).astype(BF16)
    k_out = (k * jnp.exp(b_end - b)).astype(BF16)
    d_end = jnp.exp(b_end)
    v = v_ref[...].astype(BF16)
    nt = (((1,), (1,)), ((), ()))
    tn = (((0,), (0,)), ((), ()))
    for hd in range(GLA_H):
        ks = slice(hd * GLA_HK, (hd + 1) * GLA_HK)
        vs = slice(hd * GLA_HV, (hd + 1) * GLA_HV)
        scores = lax.dot_general(q_intra[:, ks], k_intra[:, ks], nt, preferred_element_type=F32)
        scores = jnp.where(causal, scores, 0.0).astype(BF16)
        state = s_ref[hd]
        o_h = _dot(scores, v[:, vs]) + lax.dot_general(q_in[:, ks], state.astype(BF16), nt,
                                                       preferred_element_type=F32)
        o_ref[:, vs] = o_h
        s_ref[hd] = state * d_end[:, ks] + lax.dot_general(v[:, vs], k_out[:, ks], tn,
                                                           preferred_element_type=F32)


def _gla_kernel(qf_ref, kf_ref, vf_ref, gf_ref, qb_ref, kb_ref, vb_ref, gb_ref, s0f_ref, s0b_ref,
                of_ref, ob_ref, sff_ref, sfb_ref, sf_sc, sb_sc):
    @pl.when(pl.program_id(1) == 0)
    def _():
        sf_sc[...] = s0f_ref[...]
        sb_sc[...] = s0b_ref[...]

    for n in range(GLA_NB):
        _gla_chunk(qf_ref.at[n], kf_ref.at[n], vf_ref.at[n], gf_ref.at[n], of_ref.at[n], sf_sc.at[n], False)
        _gla_chunk(qb_ref.at[n], kb_ref.at[n], vb_ref.at[n], gb_ref.at[n], ob_ref.at[n], sb_sc.at[n], True)

    @pl.when(pl.program_id(1) == pl.num_programs(1) - 1)
    def _():
        sff_ref[...] = sf_sc[...]
        sfb_ref[...] = sb_sc[...]


def _gla_call(q, k, v, g_f, g_b, s0_f, s0_b):
    C = GLA_CHUNK
    n_chunk = q.shape[1] // C

    def spec(width, reverse):
        return pl.BlockSpec((GLA_NB, C, width), lambda b, i: (b, (n_chunk - 1 - i) if reverse else i, 0))

    state = pl.BlockSpec((GLA_NB, GLA_H, GLA_HV, GLA_HK), lambda b, i: (b, 0, 0, 0))
    ins = lambda reverse: [spec(GLA_DK, reverse), spec(GLA_DK, reverse), spec(GLA_DV, reverse),
                           spec(GLA_DK, reverse)]
    o_shape = jax.ShapeDtypeStruct((BATCH, q.shape[1], GLA_DV), F32)
    s_shape = jax.ShapeDtypeStruct((BATCH, GLA_H, GLA_HV, GLA_HK), F32)
    return pl.pallas_call(
        _gla_kernel,
        grid=(BATCH // GLA_NB, n_chunk),
        in_specs=ins(False) + ins(True) + [state, state],
        out_specs=[spec(GLA_DV, False), spec(GLA_DV, True), state, state],
        out_shape=[o_shape, o_shape, s_shape, s_shape],
        scratch_shapes=[pltpu.VMEM((GLA_NB, GLA_H, GLA_HV, GLA_HK), F32)] * 2,
        compiler_params=_cparams(2),
        name="gla_scan",
    )(q, k, v, g_f, q, k, v, g_b, s0_f, s0_b)


def _odd_out_kernel(h_ref, m_ref, of_ref, ob_ref, og_ref, ng_ref, w_ref, lg_ref, lb_ref,
                    wg_ref, wu_ref, wd_ref, lg2_ref, lb2_ref, o_ref):
    o = of_ref[...] + ob_ref[...]
    og = og_ref[...]
    parts = []
    for hd in range(GLA_H):
        o_h = o[:, hd * GLA_HV:(hd + 1) * GLA_HV]
        ms = jnp.mean(o_h * o_h, axis=-1, keepdims=True)
        parts.append(o_h * lax.rsqrt(ms + GLA_EPS) * ng_ref[...])
    y = jnp.concatenate(parts, axis=-1) * _silu(og)
    mix = _dot(y.astype(BF16), w_ref[...])
    r = DN_ALPHA * h_ref[...] + m_ref[5:6, :] * mix
    h_mid = _layer_norm_rows(r, lg_ref[...], lb_ref[...])
    o_ref[...] = _ffn_rows(h_mid, m_ref, 6, wg_ref, wu_ref, wd_ref, lg2_ref, lb2_ref)


def _odd_out_call(st, h, mods, layer, o_f, o_b, og, norm_g, w_out, ln_g, ln_b, ffn_weights, ln_g2, ln_b2):
    return pl.pallas_call(
        _odd_out_kernel,
        grid=st.grid,
        in_specs=[
            _tok_spec(st, D_MODEL), _mod_spec(layer, st),
            _tok_spec(st, GLA_DV), _tok_spec(st, GLA_DV), _tok_spec(st, GLA_DV),
            _const_spec((1, GLA_HV)), _const_spec((GLA_DV, D_MODEL)),
            _const_spec((1, D_MODEL)), _const_spec((1, D_MODEL)),
        ] + _ffn_specs(),
        out_specs=_tok_spec(st, D_MODEL),
        out_shape=_tok_shape(st, D_MODEL),
        compiler_params=_cparams(2),
        name="odd_out_ffn",
    )(h, mods, o_f, o_b, og, norm_g[None, :], w_out, ln_g, ln_b, *ffn_weights, ln_g2, ln_b2)


def _raster_to_columns(a):
    return a.reshape(BATCH, SEQ // GRID_W, GRID_W, a.shape[-1]).swapaxes(1, 2).reshape(a.shape)


def _columns_to_raster(a):
    return a.reshape(BATCH, GRID_W, SEQ // GRID_W, a.shape[-1]).swapaxes(1, 2).reshape(a.shape)


def kernel(x, c, ctx, c_ctx, ada_w, ada_b, ln_g, ln_b, ffn_wg, ffn_wu, ffn_wd, ev_w_in, ev_w_out, hy_conv_w, hy_conv_b, hy_f_w1, hy_f_b1, hy_f_freq1, hy_f_w2, hy_f_b2, hy_f_freq2, hy_f_w3, hy_skip, rw_mu, rw_w0, rw_w_up, rw_a0, rw_a_up, rw_g_up, rw_k_k, rw_k_a, rw_r_k, rw_gn_g, rw_gn_b, od_w_in, od_w_out, gla_g_up, gla_g_b, gla_norm_g):
    h_lat, h_ctx = x, _pair_ctx(ctx)
    s = jnp.concatenate([c, c_ctx[None, :], jnp.zeros((MOD_ROWS - BATCH - 1, D_MODEL), F32)], axis=0)
    mods = _ada_call(s, ada_w, ada_b).reshape(DEPTH, MOD_ROWS, 9, D_MODEL)
    dft_lat, dft_ctx = _block_dft_tables(min(HY_S, SEQ)), _block_dft_tables(min(HY_S, CTX_LEN))

    for l in range(DEPTH):
        last = l == DEPTH - 1
        lg = lambda i: ln_g[l, i][None, :]
        lb = lambda i: ln_b[l, i][None, :]
        w_ffn = _ffn_weights(ffn_wg[l, 0], ffn_wu[l, 0], ffn_wd[l, 0])
        h_lat = _ffn_call(LAT, h_lat, mods, l, w_ffn, lg(0), lb(0))
        h_ctx = _ffn_call(CTX, h_ctx, mods, l, w_ffn, lg(0), lb(0))
        ffn2 = (_ffn_weights(ffn_wg[l, 1], ffn_wu[l, 1], ffn_wd[l, 1]), lg(2), lb(2))

        if l % 2 == 0:
            e = l // 2
            w_in = ev_w_in[e].astype(BF16)
            w_out = ev_w_out[e].astype(BF16)
            hy_params = (hy_conv_w[e], hy_conv_b[e][None, :], hy_f_w1[e], hy_f_b1[e], hy_f_freq1[e], hy_f_w2[e],
                         hy_f_b2[e], hy_f_freq2[e], hy_f_w3[e], hy_skip[e])
            rw_params = _rw_prep_params(rw_mu[e], rw_k_k[e], rw_k_a[e], rw_r_k[e], rw_w0[e], rw_a0[e],
                                        rw_w_up[e], rw_a_up[e], rw_g_up[e])
            zl_hy, zl_rw = _inproj_call(LAT, h_lat, mods, l, w_in, (HY_IN, RW_IN))
            zc_hy, zc_rw = _inproj_call(CTX, h_ctx, mods, l, w_in, (HY_IN, RW_IN))
            hy_lat = _hyena(zl_hy, SEQ, dft_lat, *hy_params)
            hy_ctx = _hyena(_unpair_ctx(zc_hy), CTX_LEN, dft_ctx, *hy_params)
            prep_lat = _rw_prep_call(LAT_SEG, zl_rw, rw_params)
            prep_ctx = _rw_prep_call(CTX_SEG, _unpair_ctx(zc_rw), rw_params)
            o_ctx, o_lat = _rwkv_bidir(prep_ctx, prep_lat)
            h_lat = _even_out_call(LAT, h_lat, mods, l, hy_lat, o_lat, prep_lat[4], prep_lat[3],
                                   rw_gn_g[e], rw_gn_b[e], w_out, lg(1), lb(1), *ffn2)
            if not last:
                h_ctx = _even_out_call(CTX, h_ctx, mods, l, _pair_ctx(hy_ctx), _pair_ctx(o_ctx),
                                       _pair_ctx(prep_ctx[4]), _pair_ctx(prep_ctx[3]),
                                       rw_gn_g[e], rw_gn_b[e], w_out, lg(1), lb(1), *ffn2)
        else:
            o = l // 2
            w = od_w_in[o]
            n_qkv = 2 * GLA_DK + GLA_DV
            n_gate = 2 * GLA_GATE_LORA
            w_in = jnp.concatenate([w[:, :n_qkv], w[:, n_qkv + n_gate:], w[:, n_qkv:n_qkv + n_gate],
                                    jnp.zeros((D_MODEL, GLA_GPAD - n_gate), F32)], axis=1).astype(BF16)
            w_out = od_w_out[o].astype(BF16)
            pad = jnp.zeros((GLA_GPAD - n_gate, GLA_DK), F32)
            zero = jnp.zeros((GLA_GATE_LORA, GLA_DK), F32)
            gup = jnp.concatenate([jnp.concatenate([gla_g_up[o, 0], zero, pad], axis=0),
                                   jnp.concatenate([zero, gla_g_up[o, 1], pad], axis=0)], axis=1)
            g_bias = gla_g_b[o].reshape(1, 2 * GLA_DK)
            h_lat = _raster_to_columns(h_lat)
            ql, kl, vl, ogl, gfl, gbl = _odd_inproj_call(LAT, h_lat, mods, l, w_in, gup, g_bias)
            qc, kc, vc, ogc, gfc, gbc = [_unpair_ctx(a)
                                         for a in _odd_inproj_call(CTX, h_ctx, mods, l, w_in, gup, g_bias)]
            zero_state = jnp.zeros((BATCH, GLA_H, GLA_HV, GLA_HK), F32)
            oc_f, oc_b, s_f, s_b = _gla_call(qc, kc, vc, gfc, gbc, zero_state, zero_state)
            ol_f, ol_b, _, _ = _gla_call(ql, kl, vl, gfl, gbl, s_f, s_b)
            h_lat = _odd_out_call(LAT, h_lat, mods, l, ol_f, ol_b, ogl, gla_norm_g[o], w_out, lg(1), lb(1),
                                  *ffn2)
            h_lat = _columns_to_raster(h_lat)
            if not last:
                h_ctx = _odd_out_call(CTX, h_ctx, mods, l, _pair_ctx(oc_f), _pair_ctx(oc_b),
                                      _pair_ctx(ogc), gla_norm_g[o], w_out, lg(1), lb(1), *ffn2)
    return h_lat
```

```python
import functools
import math
from typing import NamedTuple

import jax
import jax.numpy as jnp
from jax import lax
from jax.experimental import pallas as pl
from jax.experimental.pallas import tpu as pltpu

F32 = jnp.float32
BF16 = jnp.bfloat16
HIGHEST = lax.Precision.HIGHEST

D_MODEL = 1024
BATCH = 16
SEQ = 2048
DEPTH = 4
GRID_W = 64
CTX_LEN = 256
DN_ALPHA = (2 * DEPTH) ** 0.25
LN_EPS = 1e-6
D_FF = 2816

HY_W = D_MODEL // 2
HY_ORDER = 2
HY_IN = (HY_ORDER + 1) * HY_W
HY_EMB = 33
HY_FO = 64
HY_TARGET = 1e-2
HY_FAST = 0.3
HY_SLOW = 1.5

RW_W = D_MODEL - HY_W
RW_N = 64
RW_H = RW_W // RW_N
RW_DECAY_LORA = 64
RW_A_LORA = 64
RW_GATE_LORA = 128
RW_GN_EPS = 64e-5
RW_IN = 3 * RW_W + 2 * RW_DECAY_LORA + 2 * RW_A_LORA + RW_GATE_LORA
EV_IN = HY_IN + RW_IN

GLA_H = 4
GLA_DK = D_MODEL // 2
GLA_DV = D_MODEL
GLA_HK = GLA_DK // GLA_H
GLA_HV = GLA_DV // GLA_H
GLA_GATE_LORA = 16
GLA_NORMALIZER = 16.0
GLA_CHUNK = 64
GLA_EPS = 1e-5

LANES = 128
SUBLANES = 8

MOD_ROWS = 24
TF = 256
NF = D_FF // TF
ADA_COL_BLOCKS = 4
SCAN_TB = 64
BH = BATCH * RW_H
HY_S = 512
HY_CT = 256
HY_MAC_ROWS = 32
VMEM_LIMIT = 56 * 1024 * 1024


class _Stream(NamedTuple):
    groups: int
    rows: int
    tm: int
    ctx: bool

    @property
    def grid(self):
        return (self.groups, self.rows // self.tm)


TM_WIDE = 2 * CTX_LEN
TM_SEG = CTX_LEN
LAT = _Stream(BATCH, SEQ, TM_WIDE, False)
CTX = _Stream(BATCH // 2, 2 * CTX_LEN, TM_WIDE, True)
LAT_SEG = _Stream(BATCH, SEQ, TM_SEG, False)
CTX_SEG = _Stream(BATCH, CTX_LEN, TM_SEG, True)


def _pair_ctx(a):
    return a.reshape(BATCH // 2, 2 * CTX_LEN, a.shape[-1])


def _unpair_ctx(a):
    return a.reshape(BATCH, CTX_LEN, a.shape[-1])


def _cparams(n_axes):
    return pltpu.CompilerParams(dimension_semantics=("arbitrary",) * n_axes,
                                vmem_limit_bytes=VMEM_LIMIT)


def _layer_norm_rows(r, g, b):
    mu = jnp.mean(r, axis=-1, keepdims=True)
    xc = r - mu
    var = jnp.mean(xc * xc, axis=-1, keepdims=True)
    return xc * lax.rsqrt(var + LN_EPS) * g + b


def _silu(x):
    return x * jax.nn.sigmoid(x)


def _split_bf16(x):
    hi = x.astype(BF16)
    lo = (x - hi.astype(F32)).astype(BF16)
    return hi, lo


def _dot(a, b):
    return jnp.dot(a, b, preferred_element_type=F32)


def _dot_f32(a, b):
    return jnp.dot(a, b, preferred_element_type=F32, precision=HIGHEST)


def _dot3(a_hi, a_lo, b_hi, b_lo):
    return _dot(a_hi, b_hi) + (_dot(a_lo, b_hi) + _dot(a_hi, b_lo))


def _dot_x3(a, b):
    return _dot3(*_split_bf16(a), *_split_bf16(b))


def _mod_spec(layer, st):
    def index(g, t):
        return (layer, BATCH if st.ctx else g, 0, 0)
    return pl.BlockSpec((None, None, 9, D_MODEL), index)


def _tok_spec(st, width, col=0):
    return pl.BlockSpec((None, st.tm, width), lambda g, t: (g, t, col))


def _const_spec(shape):
    nd = len(shape)
    return pl.BlockSpec(shape, lambda g, t: (0,) * nd, pipeline_mode=pl.Buffered(1))


def _tok_shape(st, width):
    return jax.ShapeDtypeStruct((st.groups, st.rows, width), F32)


def _ada_kernel(s_ref, w_ref, b_ref, o_ref):
    s = _silu(s_ref[...])
    o_ref[...] = _dot_f32(s, w_ref[...]) + b_ref[...]


def _ada_call(s, ada_w, ada_b):
    n_blk = ADA_COL_BLOCKS
    tn = 9 * D_MODEL // n_blk
    return pl.pallas_call(
        _ada_kernel,
        grid=(DEPTH, n_blk),
        in_specs=[
            pl.BlockSpec((MOD_ROWS, D_MODEL), lambda l, j: (0, 0)),
            pl.BlockSpec((None, D_MODEL, tn), lambda l, j: (l, 0, j)),
            pl.BlockSpec((None, 1, tn), lambda l, j: (l, 0, j)),
        ],
        out_specs=pl.BlockSpec((None, MOD_ROWS, tn), lambda l, j: (l, 0, j)),
        out_shape=jax.ShapeDtypeStruct((DEPTH, MOD_ROWS, 9 * D_MODEL), F32),
        compiler_params=_cparams(2),
        name="ada_mod",
    )(s, ada_w, ada_b.reshape(DEPTH, 1, 9 * D_MODEL))


def _ffn_rows(h, m_ref, mi, wg_ref, wu_ref, wd_ref, g_ref, b_ref):
    shift = m_ref[mi:mi + 1, :]
    scale = m_ref[mi + 1:mi + 2, :]
    gate = m_ref[mi + 2:mi + 3, :]
    hm = (h * (1.0 + scale) + shift).astype(BF16)
    acc = jnp.zeros(h.shape, F32)
    for j in range(NF):
        a = _silu(_dot(hm, wg_ref[j])) * _dot(hm, wu_ref[j])
        acc = acc + _dot(a.astype(BF16), wd_ref[j])
    r = DN_ALPHA * h + (0.5 * gate) * acc
    return _layer_norm_rows(r, g_ref[...], b_ref[...])


def _ffn_specs():
    return [_const_spec((NF, D_MODEL, TF)), _const_spec((NF, D_MODEL, TF)), _const_spec((NF, TF, D_MODEL)),
            _const_spec((1, D_MODEL)), _const_spec((1, D_MODEL))]


def _ffn_kernel(h_ref, m_ref, wg_ref, wu_ref, wd_ref, g_ref, b_ref, o_ref):
    o_ref[...] = _ffn_rows(h_ref[...], m_ref, 0, wg_ref, wu_ref, wd_ref, g_ref, b_ref)


def _ffn_call(st, h, mods, layer, weights, ln_g, ln_b):
    return pl.pallas_call(
        _ffn_kernel,
        grid=st.grid,
        in_specs=[_tok_spec(st, D_MODEL), _mod_spec(layer, st)] + _ffn_specs(),
        out_specs=_tok_spec(st, D_MODEL),
        out_shape=_tok_shape(st, D_MODEL),
        compiler_params=_cparams(2),
        name="ffn",
    )(h, mods, *weights, ln_g, ln_b)


def _ffn_weights(wg, wu, wd):
    wg = wg.astype(BF16).reshape(D_MODEL, NF, TF).transpose(1, 0, 2)
    wu = wu.astype(BF16).reshape(D_MODEL, NF, TF).transpose(1, 0, 2)
    wd = wd.astype(BF16).reshape(NF, TF, D_MODEL)
    return wg, wu, wd


def _inproj_kernel(h_ref, m_ref, w_ref, *o_refs, widths):
    u = (h_ref[...] * (1.0 + m_ref[4:5, :]) + m_ref[3:4, :]).astype(BF16)
    off = 0
    for o_ref, wd in zip(o_refs, widths):
        o_ref[...] = _dot(u, w_ref[:, off:off + wd])
        off += wd


def _inproj_call(st, h, mods, layer, w, widths):
    n_in = sum(widths)
    return pl.pallas_call(
        functools.partial(_inproj_kernel, widths=widths),
        grid=st.grid,
        in_specs=[_tok_spec(st, D_MODEL), _mod_spec(layer, st), _const_spec((D_MODEL, n_in))],
        out_specs=[_tok_spec(st, wd) for wd in widths],
        out_shape=[_tok_shape(st, wd) for wd in widths],
        compiler_params=_cparams(2),
        name="mixer_in",
    )(h, mods, w)


def _hyena_feature_table(L):
    t = jnp.linspace(0.0, 1.0, L, dtype=F32)[:, None]
    bands = (HY_EMB - 1) // 2
    w = 2.0 * math.pi * jnp.arange(L, dtype=F32)[:, None] / L
    f = jnp.linspace(1e-4, bands - 1, bands, dtype=F32)[None, :]
    z = jnp.concatenate([t, jnp.cos(f * w), -jnp.sin(f * w)], axis=-1)
    z = jnp.pad(z, ((0, 0), (0, LANES - HY_EMB)))
    deltas = jnp.linspace(math.log(HY_TARGET) / HY_SLOW, math.log(HY_TARGET) / HY_FAST, HY_W, dtype=F32)
    neg = (L - jnp.arange(L)) % L
    return z, t, z[neg], t[neg], jnp.abs(deltas)[None, :]


def _hy_filter_kernel(zp_ref, tp_ref, zn_ref, tn_ref, d_ref, w1_ref, b1_ref, f1_ref, w2_ref, b2_ref, f2_ref,
                      w3_ref, g_ref, *, L):
    def hidden(z):
        h = jnp.sin(f1_ref[...] * (_dot_f32(z, w1_ref[...]) + b1_ref[...]))
        return jnp.sin(f2_ref[...] * (_dot_f32(h, w2_ref[...]) + b2_ref[...]))

    h_pos = hidden(zp_ref[...])
    h_neg = hidden(zn_ref[...])
    win_pos = jnp.exp(-tp_ref[...] * d_ref[...])
    win_neg = jnp.exp(-tn_ref[...] * d_ref[...])
    rows = lax.broadcasted_iota(jnp.int32, win_neg.shape, 0)
    for o in range(HY_ORDER):
        base = o * 2 * HY_W
        g_ref[o, L:2 * L, :] = _dot_f32(h_pos, w3_ref[:, base:base + HY_W]) * win_pos
        h_bwd = _dot_f32(h_neg, w3_ref[:, base + HY_W:base + 2 * HY_W]) * win_neg
        g_ref[o, 0:L, :] = jnp.where(rows == 0, 0.0, h_bwd)


def _hy_filter_call(L, w1, b1, f1, w2, b2, f2, w3):
    zp, tp, zn, tn, deltas = _hyena_feature_table(L)
    w1p = jnp.pad(w1, ((0, LANES - HY_EMB), (0, 0)))
    return pl.pallas_call(
        functools.partial(_hy_filter_kernel, L=L),
        out_shape=jax.ShapeDtypeStruct((HY_ORDER, 2 * L, HY_W), F32),
        compiler_params=pltpu.CompilerParams(vmem_limit_bytes=VMEM_LIMIT),
        name="hyena_filter",
    )(zp, tp, zn, tn, deltas, w1p, b1[None, :], f1[None, :], w2, b2[None, :], f2[None, :], w3)


def _block_dft_tables(s):
    n = 2 * s
    f = jnp.arange(s, dtype=jnp.int32)[:, None]

    def trig(cols):
        j = jnp.arange(cols, dtype=jnp.int32)[None, :]
        ang = ((f * j) % n).astype(F32) * (2.0 * math.pi / n)
        alt = jnp.where(j % 2 == 0, 1.0, -1.0).astype(F32)
        return j, jnp.cos(ang), jnp.sin(ang), alt

    _, cos, sin, alt = trig(s)
    fwd = jnp.concatenate([cos, jnp.where(f == 0, alt, -sin)], axis=0)
    scale = jnp.where(f == 0, 1.0 / n, 2.0 / n).astype(F32)
    inv = jnp.concatenate([(scale * cos).T, jnp.where(f == 0, alt / n, -scale * sin).T], axis=1)
    j, cos, sin, alt = trig(n)
    sgn = jnp.where(f % 2 == 0, 1.0, -1.0).astype(F32)
    taps = jnp.concatenate([sgn * cos, jnp.where(f == 0, alt, -sgn * sin)], axis=0)
    taps = jnp.where(j == 0, 0.0, taps)
    return _split_bf16(fwd), _split_bf16(inv), _split_bf16(taps)


def _hy_spectrum_kernel(th_ref, tl_ref, lo_ref, hi_ref, k_ref):
    s = lo_ref.shape[0]
    a_hi, a_lo = _split_bf16(lo_ref[...])
    b_hi, b_lo = _split_bf16(hi_ref[...])
    k_ref[...] = (_dot3(th_ref[:, :s], tl_ref[:, :s], a_hi, a_lo)
                  + _dot3(th_ref[:, s:], tl_ref[:, s:], b_hi, b_lo))


def _hy_spectrum_call(L, taps_tab, taps):
    th, tl = taps_tab
    s = th.shape[0] // 2
    nd = 2 * (L // s) - 1
    return pl.pallas_call(
        _hy_spectrum_kernel,
        grid=(HY_ORDER, nd),
        in_specs=[
            pl.BlockSpec((2 * s, 2 * s), lambda o, d: (0, 0)),
            pl.BlockSpec((2 * s, 2 * s), lambda o, d: (0, 0)),
            pl.BlockSpec((None, s, HY_W), lambda o, d: (o, d, 0)),
            pl.BlockSpec((None, s, HY_W), lambda o, d: (o, d + 1, 0)),
        ],
        out_specs=pl.BlockSpec((None, None, 2 * s, HY_W), lambda o, d: (o, d, 0, 0)),
        out_shape=jax.ShapeDtypeStruct((HY_ORDER, nd, 2 * s, HY_W), F32),
        compiler_params=_cparams(2),
        name="hyena_spectrum",
    )(th, tl, taps, taps)


def _conv3_rows(z, w, b):
    n = z.shape[0]
    rows = lax.broadcasted_iota(jnp.int32, z.shape, 0)
    prev = jnp.where(rows == 0, 0.0, pltpu.roll(z, 1, 0))
    nxt = jnp.where(rows == n - 1, 0.0, pltpu.roll(z, n - 1, 0))
    return prev * w[0:1, :] + z * w[1:2, :] + nxt * w[2:3, :] + b


def _hy_conv_kernel(u_ref, x_ref, cwu_ref, cbu_ref, cwx_ref, cbx_ref, skip_ref, k_ref, fwd_ref, inv_ref,
                    o_ref, u_sc, xf_ref, yf_ref, *, m, conv_u):
    s = inv_ref.shape[0]
    ct = o_ref.shape[-1]
    if conv_u:
        u_sc[...] = _conv3_rows(u_ref[...], cwu_ref[...], cbu_ref[...])
        src = u_sc
    else:
        src = u_ref
    o_ref[...] = _conv3_rows(x_ref[...], cwx_ref[...], cbx_ref[...])

    for jb in range(m):
        xf_ref[jb] = _dot(fwd_ref[...], src[jb * s:(jb + 1) * s, :].astype(BF16))

    for ib in range(m):
        def chunk(r, carry):
            re = pl.ds(pl.multiple_of(r * HY_MAC_ROWS, HY_MAC_ROWS), HY_MAC_ROWS)
            im = pl.ds(pl.multiple_of(s + r * HY_MAC_ROWS, HY_MAC_ROWS), HY_MAC_ROWS)
            acc_re = jnp.zeros((HY_MAC_ROWS, ct), F32)
            acc_im = jnp.zeros((HY_MAC_ROWS, ct), F32)
            for jb in range(m):
                d = ib - jb + m - 1
                x_re, x_im = xf_ref[jb, re, :], xf_ref[jb, im, :]
                k_re, k_im = k_ref[d, re, :], k_ref[d, im, :]
                acc_re = acc_re + (x_re * k_re - x_im * k_im)
                acc_im = acc_im + (x_re * k_im + x_im * k_re)
            yf_ref[ib, re, :] = acc_re
            yf_ref[ib, im, :] = acc_im
            return carry

        lax.fori_loop(0, s // HY_MAC_ROWS, chunk, 0)
        dc = jnp.zeros((1, ct), F32)
        ny = jnp.zeros((1, ct), F32)
        for jb in range(m):
            d = ib - jb + m - 1
            dc = dc + xf_ref[jb, 0:1, :] * k_ref[d, 0:1, :]
            ny = ny + xf_ref[jb, s:s + 1, :] * k_ref[d, s:s + 1, :]
        yf_ref[ib, 0:1, :] = dc
        yf_ref[ib, s:s + 1, :] = ny

    skip = skip_ref[...]
    for ib in range(m):
        conv = _dot(inv_ref[...], yf_ref[ib].astype(BF16))
        rows = slice(ib * s, (ib + 1) * s)
        o_ref[rows, :] = o_ref[rows, :] * (conv + skip * src[rows, :])


def _hy_conv_call(u, u_part, z_hy, x_part, L, order, conv_u, conv_w, conv_b, skip, spectrum, fwd_tab, inv_tab):
    s = inv_tab[0].shape[0]
    m = L // s
    nd = 2 * m - 1
    ct = HY_CT
    nct = HY_W // ct
    fwd, inv = fwd_tab[0], inv_tab[0]

    def cspec(rows, part):
        return pl.BlockSpec((rows, ct), lambda c, b: (0, part * nct + c))

    tab = lambda shape: pl.BlockSpec(shape, lambda c, b: (0, 0))
    return pl.pallas_call(
        functools.partial(_hy_conv_kernel, m=m, conv_u=conv_u),
        grid=(nct, BATCH),
        in_specs=[
            pl.BlockSpec((None, L, ct), lambda c, b: (b, 0, u_part * nct + c)),
            pl.BlockSpec((None, L, ct), lambda c, b: (b, 0, x_part * nct + c)),
            cspec(3, u_part), cspec(1, u_part), cspec(3, x_part), cspec(1, x_part),
            pl.BlockSpec((None, 1, ct), lambda c, b: (order, 0, c)),
            pl.BlockSpec((None, nd, 2 * s, ct), lambda c, b: (order, 0, 0, c)),
            tab((2 * s, s)), tab((s, 2 * s)),
        ],
        out_specs=pl.BlockSpec((None, L, ct), lambda c, b: (b, 0, c)),
        out_shape=jax.ShapeDtypeStruct((BATCH, L, HY_W), F32),
        scratch_shapes=[
            pltpu.VMEM((L, ct), F32),
            pltpu.VMEM((m, 2 * s, ct), F32),
            pltpu.VMEM((m, 2 * s, ct), F32),
        ],
        compiler_params=_cparams(2),
        name="hyena_conv",
    )(u, z_hy, conv_w, conv_b, conv_w, conv_b, skip[:, None, :], spectrum, fwd, inv)


def _hyena(z_hy, L, tables, conv_w, conv_b, f_w1, f_b1, f_freq1, f_w2, f_b2, f_freq2, f_w3, skip):
    fwd_tab, inv_tab, taps_tab = tables
    taps = _hy_filter_call(L, f_w1, f_b1, f_freq1, f_w2, f_b2, f_freq2, f_w3)
    spectrum = _hy_spectrum_call(L, taps_tab, taps)
    y1 = _hy_conv_call(z_hy, 0, z_hy, 1, L, 0, True, conv_w, conv_b, skip, spectrum, fwd_tab, inv_tab)
    return _hy_conv_call(y1, 0, z_hy, 2, L, 1, False, conv_w, conv_b, skip, spectrum, fwd_tab, inv_tab)


def _head_sum_matrix(width, head):
    i = jnp.arange(width)[:, None] // head
    j = jnp.arange(width)[None, :] // head
    return (i == j).astype(BF16)


def _group_sum(x, ones_ref):
    hi, lo = _split_bf16(x)
    return _dot(hi, ones_ref[...]) + _dot(lo, ones_ref[...])


def _rw_prep_kernel(z_ref, zp_ref, zn_ref, mu_ref, kk_ref, ka_ref, rk_ref, w0_ref, a0_ref, wup_ref, aup_ref,
                    gup_ref, ones_ref,
                    r_ref, v_ref, nkk_ref, g_ref, bonus_ref, wf_ref, kf_ref, af_ref, wb_ref, kb_ref, ab_ref):
    t = pl.program_id(1)
    z = z_ref[...]
    tm = z.shape[0]
    prev_row = jnp.where(t == 0, 0.0, zp_ref[SUBLANES - 1:SUBLANES, :])
    next_row = jnp.where(t == pl.num_programs(1) - 1, 0.0, zn_ref[0:1, :])
    rows = lax.broadcasted_iota(jnp.int32, z.shape, 0)
    prev = jnp.where(rows == 0, prev_row, pltpu.roll(z, 1, 0))
    nxt = jnp.where(rows == tm - 1, next_row, pltpu.roll(z, tm - 1, 0))
    z = z + mu_ref[...] * (0.5 * (prev + nxt) - z)

    W = RW_W
    r = z[:, 0:W]
    k = z[:, W:2 * W]
    v = z[:, 2 * W:3 * W]
    o = 3 * W
    w_lora = _dot_x3(jnp.tanh(z[:, o:o + 2 * RW_DECAY_LORA]), wup_ref[...])
    o += 2 * RW_DECAY_LORA
    a_lora = _dot_x3(z[:, o:o + 2 * RW_A_LORA], aup_ref[...])
    o += 2 * RW_A_LORA
    g = _dot_x3(jax.nn.sigmoid(z[:, o:o + RW_GATE_LORA]), gup_ref[...])

    kk = k * kk_ref[...]
    norm = jnp.sqrt(_group_sum(kk * kk, ones_ref))
    kk = kk / jnp.maximum(norm, 1e-12)

    k_sum = jnp.zeros_like(k)
    for d, (w_ref, kd_ref, ad_ref) in enumerate(((wf_ref, kf_ref, af_ref), (wb_ref, kb_ref, ab_ref))):
        y = -(w0_ref[d:d + 1, :] + w_lora[:, d * W:(d + 1) * W])
        softplus = jnp.maximum(y, 0.0) + jnp.log(1.0 + jnp.exp(-jnp.abs(y)))
        w_log = -softplus - 0.5
        a = jax.nn.sigmoid(a0_ref[d:d + 1, :] + a_lora[:, d * W:(d + 1) * W])
        kd = k * (1.0 + (a - 1.0) * ka_ref[...])
        w_ref[...] = jnp.exp(-jnp.exp(w_log))
        kd_ref[...] = kd
        ad_ref[...] = kk * a
        k_sum = k_sum + kd

    r_ref[...] = r
    v_ref[...] = v
    nkk_ref[...] = -kk
    g_ref[...] = g
    bonus_ref[...] = _group_sum(r * k_sum * rk_ref[...], ones_ref) * v


def _rw_prep_call(st, z_rw, params):
    W = RW_W
    blocks_per_tile = st.tm // SUBLANES
    n_row_blocks = st.rows // SUBLANES
    return pl.pallas_call(
        _rw_prep_kernel,
        grid=st.grid,
        in_specs=[
            _tok_spec(st, RW_IN),
            pl.BlockSpec((None, SUBLANES, RW_IN),
                         lambda g, t: (g, jnp.maximum(t * blocks_per_tile - 1, 0), 0)),
            pl.BlockSpec((None, SUBLANES, RW_IN),
                         lambda g, t: (g, jnp.minimum((t + 1) * blocks_per_tile, n_row_blocks - 1), 0)),
            _const_spec((1, RW_IN)),
            _const_spec((1, W)), _const_spec((1, W)), _const_spec((1, W)),
            _const_spec((2, W)), _const_spec((2, W)),
            _const_spec((2 * RW_DECAY_LORA, 2 * W)), _const_spec((2 * RW_A_LORA, 2 * W)),
            _const_spec((RW_GATE_LORA, W)),
            _const_spec((W, W)),
        ],
        out_specs=[_tok_spec(st, W)] * 11,
        out_shape=[_tok_shape(st, W)] * 11,
        compiler_params=_cparams(2),
        name="rwkv_prepare",
    )(z_rw, z_rw, z_rw, *params)


def _rw_prep_params(mu, k_k, k_a, r_k, w0, a0, w_up, a_up, g_up):
    W = RW_W
    zero = jnp.zeros((RW_DECAY_LORA, W), F32)
    wup = jnp.concatenate([jnp.concatenate([w_up[0], zero], 1), jnp.concatenate([zero, w_up[1]], 1)], 0)
    aup = jnp.concatenate([jnp.concatenate([a_up[0], zero], 1), jnp.concatenate([zero, a_up[1]], 1)], 0)
    return (mu[None, :], k_k[None, :], k_a[None, :], r_k.reshape(1, W), w0, a0, wup, aup, g_up,
            _head_sum_matrix(W, RW_N))


def _rw_scan_kernel(r_ref, v_ref, nkk_ref, w_ref, kd_ref, ka_ref, s0_ref, *rest, reverse, accumulate):
    if accumulate:
        oacc_ref, o_ref, sfin_ref, s_ref, sa_ref = rest
    else:
        o_ref, sfin_ref, s_ref, sa_ref = rest
    @pl.when(pl.program_id(0) == 0)
    def _():
        s_ref[...] = s0_ref[...]

    t_first = SCAN_TB - 1 if reverse else 0
    sa = jnp.zeros((RW_N, BH), F32)
    for k in range(RW_N):
        sa = sa + s_ref[k] * nkk_ref[t_first, k:k + 1, :]
    sa_ref[...] = sa

    def step(i, carry):
        t = SCAN_TB - 1 - i if reverse else i
        t_next = jnp.clip(t - 1 if reverse else t + 1, 0, SCAN_TB - 1)
        sa = sa_ref[...]
        vv = v_ref[t]
        out = jnp.zeros((RW_N, BH), F32)
        sa_next = jnp.zeros((RW_N, BH), F32)
        for k in range(RW_N):
            s_k = (s_ref[k] * w_ref[t, k:k + 1, :] + sa * ka_ref[t, k:k + 1, :]
                   + vv * kd_ref[t, k:k + 1, :])
            s_ref[k] = s_k
            out = out + s_k * r_ref[t, k:k + 1, :]
            sa_next = sa_next + s_k * nkk_ref[t_next, k:k + 1, :]
        o_ref[t] = out + oacc_ref[t] if accumulate else out
        sa_ref[...] = sa_next
        return carry

    lax.fori_loop(0, SCAN_TB, step, 0)

    @pl.when(pl.program_id(0) == pl.num_programs(0) - 1)
    def _():
        sfin_ref[...] = s_ref[...]


def _rw_scan_call(r, v, nkk, w, kd, ka, s0, reverse, o_other=None):
    n_blk = r.shape[0] // SCAN_TB
    spec = pl.BlockSpec((SCAN_TB, RW_N, BH), lambda i: ((n_blk - 1 - i) if reverse else i, 0, 0))
    state = pl.BlockSpec((RW_N, RW_N, BH), lambda i: (0, 0, 0))
    extra = [] if o_other is None else [o_other]
    return pl.pallas_call(
        functools.partial(_rw_scan_kernel, reverse=reverse, accumulate=o_other is not None),
        grid=(n_blk,),
        in_specs=[spec] * 6 + [state] + [spec] * len(extra),
        out_specs=[spec, state],
        out_shape=[jax.ShapeDtypeStruct(r.shape, F32), jax.ShapeDtypeStruct((RW_N, RW_N, BH), F32)],
        scratch_shapes=[pltpu.VMEM((RW_N, RW_N, BH), F32), pltpu.VMEM((RW_N, BH), F32)],
        compiler_params=_cparams(1),
        name="rwkv_scan",
    )(r, v, nkk, w, kd, ka, s0, *extra)


def _to_scan(x):
    n = x.shape[1]
    return x.reshape(BATCH, n, RW_H, RW_N).transpose(1, 3, 0, 2).reshape(n, RW_N, BH)


def _from_scan(x):
    n = x.shape[0]
    return x.reshape(n, RW_N, BATCH, RW_H).transpose(2, 0, 3, 1).reshape(BATCH, n, RW_W)


def _rwkv_bidir(prep_ctx, prep_lat):
    outs = {"ctx": None, "lat": None}
    shared = {name: [_to_scan(p[i]) for i in range(3)] for name, p in (("ctx", prep_ctx), ("lat", prep_lat))}
    for d, reverse in enumerate((False, True)):
        state = jnp.zeros((RW_N, RW_N, BH), F32)
        for name, p in (("ctx", prep_ctx), ("lat", prep_lat)):
            dirs = [_to_scan(p[5 + 3 * d + i]) for i in range(3)]
            outs[name], state = _rw_scan_call(*shared[name], *dirs, state, reverse, outs[name])
    return _from_scan(outs["ctx"]), _from_scan(outs["lat"])


def _even_out_kernel(h_ref, m_ref, hy_ref, osum_ref, bonus_ref, g_ref, gng_ref, gnb_ref, ones_ref,
                     why_ref, wrw_ref, lg_ref, lb_ref, wg_ref, wu_ref, wd_ref, lg2_ref, lb2_ref, o_ref):
    o = osum_ref[...]
    mu = _group_sum(o, ones_ref) * (1.0 / RW_N)
    oc = o - mu
    var = _group_sum(oc * oc, ones_ref) * (1.0 / RW_N)
    y = oc * lax.rsqrt(var + RW_GN_EPS) * gng_ref[...] + gnb_ref[...]
    rw = (y + bonus_ref[...]) * g_ref[...]
    mix = _dot(hy_ref[...].astype(BF16), why_ref[...]) + _dot(rw.astype(BF16), wrw_ref[...])
    r = DN_ALPHA * h_ref[...] + m_ref[5:6, :] * mix
    h_mid = _layer_norm_rows(r, lg_ref[...], lb_ref[...])
    o_ref[...] = _ffn_rows(h_mid, m_ref, 6, wg_ref, wu_ref, wd_ref, lg2_ref, lb2_ref)


def _even_out_call(st, h, mods, layer, hy, o_sum, bonus, g, gn_g, gn_b, w_out, ln_g, ln_b, ffn_weights,
                   ln_g2, ln_b2):
    W = RW_W
    return pl.pallas_call(
        _even_out_kernel,
        grid=st.grid,
        in_specs=[
            _tok_spec(st, D_MODEL), _mod_spec(layer, st),
            _tok_spec(st, HY_W), _tok_spec(st, W), _tok_spec(st, W), _tok_spec(st, W),
            _const_spec((1, W)), _const_spec((1, W)), _const_spec((W, W)),
            _const_spec((HY_W, D_MODEL)), _const_spec((W, D_MODEL)),
            _const_spec((1, D_MODEL)), _const_spec((1, D_MODEL)),
        ] + _ffn_specs(),
        out_specs=_tok_spec(st, D_MODEL),
        out_shape=_tok_shape(st, D_MODEL),
        compiler_params=_cparams(2),
        name="even_out_ffn",
    )(h, mods, hy, o_sum, bonus, g, gn_g[None, :], gn_b[None, :], _head_sum_matrix(W, RW_N),
      w_out[:HY_W], w_out[HY_W:], ln_g, ln_b, *ffn_weights, ln_g2, ln_b2)


GLA_GPAD = LANES
GLA_NB = 4
OD_WIDTHS = (GLA_DK, GLA_DK, GLA_DV, GLA_DV, GLA_GPAD)


def _odd_inproj_kernel(h_ref, m_ref, w_ref, gup_ref, gb_ref, q_ref, k_ref, v_ref, og_ref, gf_ref, gb_out_ref):
    u = (h_ref[...] * (1.0 + m_ref[4:5, :]) + m_ref[3:4, :]).astype(BF16)
    gd = _dot(u, w_ref[:, sum(OD_WIDTHS[:4]):])
    logit = _dot_x3(gd, gup_ref[...]) + gb_ref[...]
    log_sig = jnp.minimum(logit, 0.0) - jnp.log(1.0 + jnp.exp(-jnp.abs(logit)))
    g = log_sig * (1.0 / GLA_NORMALIZER)
    gf_ref[...] = g[:, :GLA_DK]
    gb_out_ref[...] = g[:, GLA_DK:]
    off = 0
    for o_ref, wd in zip((q_ref, k_ref, v_ref, og_ref), OD_WIDTHS[:4]):
        o_ref[...] = _dot(u, w_ref[:, off:off + wd])
        off += wd


def _odd_inproj_call(st, h, mods, layer, w, gup, g_b):
    widths = OD_WIDTHS[:4] + (GLA_DK, GLA_DK)
    return pl.pallas_call(
        _odd_inproj_kernel,
        grid=st.grid,
        in_specs=[_tok_spec(st, D_MODEL), _mod_spec(layer, st), _const_spec((D_MODEL, sum(OD_WIDTHS))),
                  _const_spec((GLA_GPAD, 2 * GLA_DK)), _const_spec((1, 2 * GLA_DK))],
        out_specs=[_tok_spec(st, wd) for wd in widths],
        out_shape=[_tok_shape(st, wd) for wd in widths],
        compiler_params=_cparams(2),
        name="mixer_in_gla",
    )(h, mods, w, gup, g_b)


def _gla_chunk(q_ref, k_ref, v_ref, g_ref, o_ref, s_ref, reverse):
    C = GLA_CHUNK
    g = g_ref[...]
    ri = lax.broadcasted_iota(jnp.int32, (C, C), 0)
    ci = lax.broadcasted_iota(jnp.int32, (C, C), 1)
    causal = (ci >= ri) if reverse else (ci <= ri)
    b = _dot_f32(causal.astype(F32), g)
    mid, end = (C - 1 - C // 2, 0) if reverse else (C // 2, C - 1)
    b_mid = b[mid:mid + 1, :]
    b_end = b[end:end + 1, :]
    q = q_ref[...] * (GLA_HK ** -0.5)
    k = k_ref[...]
    q_intra = (q * jnp.exp(b - b_mid)).astype(BF16)
    k_intra = (k * jnp.exp(b_mid - b)).astype(BF16)
    q_in = (q * jnp.exp(b)).astype(BF16)
    k_out = (k * jnp.exp(b_end - b)).astype(BF16)
    d_end = jnp.exp(b_end)
    v = v_ref[...].astype(BF16)
    nt = (((1,), (1,)), ((), ()))
    tn = (((0,), (0,)), ((), ()))
    for hd in range(GLA_H):
        ks = slice(hd * GLA_HK, (hd + 1) * GLA_HK)
        vs = slice(hd * GLA_HV, (hd + 1) * GLA_HV)
        scores = lax.dot_general(q_intra[:, ks], k_intra[:, ks], nt, preferred_element_type=F32)
        scores = jnp.where(causal, scores, 0.0).astype(BF16)
        state = s_ref[hd]
        o_h = _dot(scores, v[:, vs]) + lax.dot_general(q_in[:, ks], state.astype(BF16), nt,
                                                       preferred_element_type=F32)
        o_ref[:, vs] = o_h
        s_ref[hd] = state * d_end[:, ks] + lax.dot_general(v[:, vs], k_out[:, ks], tn,
                                                           preferred_element_type=F32)


def _gla_kernel(qf_ref, kf_ref, vf_ref, gf_ref, qb_ref, kb_ref, vb_ref, gb_ref, s0f_ref, s0b_ref,
                of_ref, ob_ref, sff_ref, sfb_ref, sf_sc, sb_sc):
    @pl.when(pl.program_id(1) == 0)
    def _():
        sf_sc[...] = s0f_ref[...]
        sb_sc[...] = s0b_ref[...]

    for n in range(GLA_NB):
        _gla_chunk(qf_ref.at[n], kf_ref.at[n], vf_ref.at[n], gf_ref.at[n], of_ref.at[n], sf_sc.at[n], False)
        _gla_chunk(qb_ref.at[n], kb_ref.at[n], vb_ref.at[n], gb_ref.at[n], ob_ref.at[n], sb_sc.at[n], True)

    @pl.when(pl.program_id(1) == pl.num_programs(1) - 1)
    def _():
        sff_ref[...] = sf_sc[...]
        sfb_ref[...] = sb_sc[...]


def _gla_call(q, k, v, g_f, g_b, s0_f, s0_b):
    C = GLA_CHUNK
    n_chunk = q.shape[1] // C

    def spec(width, reverse):
        return pl.BlockSpec((GLA_NB, C, width), lambda b, i: (b, (n_chunk - 1 - i) if reverse else i, 0))

    state = pl.BlockSpec((GLA_NB, GLA_H, GLA_HV, GLA_HK), lambda b, i: (b, 0, 0, 0))
    ins = lambda reverse: [spec(GLA_DK, reverse), spec(GLA_DK, reverse), spec(GLA_DV, reverse),
                           spec(GLA_DK, reverse)]
    o_shape = jax.ShapeDtypeStruct((BATCH, q.shape[1], GLA_DV), F32)
    s_shape = jax.ShapeDtypeStruct((BATCH, GLA_H, GLA_HV, GLA_HK), F32)
    return pl.pallas_call(
        _gla_kernel,
        grid=(BATCH // GLA_NB, n_chunk),
        in_specs=ins(False) + ins(True) + [state, state],
        out_specs=[spec(GLA_DV, False), spec(GLA_DV, True), state, state],
        out_shape=[o_shape, o_shape, s_shape, s_shape],
        scratch_shapes=[pltpu.VMEM((GLA_NB, GLA_H, GLA_HV, GLA_HK), F32)] * 2,
        compiler_params=_cparams(2),
        name="gla_scan",
    )(q, k, v, g_f, q, k, v, g_b, s0_f, s0_b)


def _odd_out_kernel(h_ref, m_ref, of_ref, ob_ref, og_ref, ng_ref, w_ref, lg_ref, lb_ref,
                    wg_ref, wu_ref, wd_ref, lg2_ref, lb2_ref, o_ref):
    o = of_ref[...] + ob_ref[...]
    og = og_ref[...]
    parts = []
    for hd in range(GLA_H):
        o_h = o[:, hd * GLA_HV:(hd + 1) * GLA_HV]
        ms = jnp.mean(o_h * o_h, axis=-1, keepdims=True)
        parts.append(o_h * lax.rsqrt(ms + GLA_EPS) * ng_ref[...])
    y = jnp.concatenate(parts, axis=-1) * _silu(og)
    mix = _dot(y.astype(BF16), w_ref[...])
    r = DN_ALPHA * h_ref[...] + m_ref[5:6, :] * mix
    h_mid = _layer_norm_rows(r, lg_ref[...], lb_ref[...])
    o_ref[...] = _ffn_rows(h_mid, m_ref, 6, wg_ref, wu_ref, wd_ref, lg2_ref, lb2_ref)


def _odd_out_call(st, h, mods, layer, o_f, o_b, og, norm_g, w_out, ln_g, ln_b, ffn_weights, ln_g2, ln_b2):
    return pl.pallas_call(
        _odd_out_kernel,
        grid=st.grid,
        in_specs=[
            _tok_spec(st, D_MODEL), _mod_spec(layer, st),
            _tok_spec(st, GLA_DV), _tok_spec(st, GLA_DV), _tok_spec(st, GLA_DV),
            _const_spec((1, GLA_HV)), _const_spec((GLA_DV, D_MODEL)),
            _const_spec((1, D_MODEL)), _const_spec((1, D_MODEL)),
        ] + _ffn_specs(),
        out_specs=_tok_spec(st, D_MODEL),
        out_shape=_tok_shape(st, D_MODEL),
        compiler_params=_cparams(2),
        name="odd_out_ffn",
    )(h, mods, o_f, o_b, og, norm_g[None, :], w_out, ln_g, ln_b, *ffn_weights, ln_g2, ln_b2)


def _raster_to_columns(a):
    return a.reshape(BATCH, SEQ // GRID_W, GRID_W, a.shape[-1]).swapaxes(1, 2).reshape(a.shape)


def _columns_to_raster(a):
    return a.reshape(BATCH, GRID_W, SEQ // GRID_W, a.shape[-1]).swapaxes(1, 2).reshape(a.shape)


def kernel(x, c, ctx, c_ctx, ada_w, ada_b, ln_g, ln_b, ffn_wg, ffn_wu, ffn_wd, ev_w_in, ev_w_out, hy_conv_w, hy_conv_b, hy_f_w1, hy_f_b1, hy_f_freq1, hy_f_w2, hy_f_b2, hy_f_freq2, hy_f_w3, hy_skip, rw_mu, rw_w0, rw_w_up, rw_a0, rw_a_up, rw_g_up, rw_k_k, rw_k_a, rw_r_k, rw_gn_g, rw_gn_b, od_w_in, od_w_out, gla_g_up, gla_g_b, gla_norm_g):
    h_lat, h_ctx = x, _pair_ctx(ctx)
    s = jnp.concatenate([c, c_ctx[None, :], jnp.zeros((MOD_ROWS - BATCH - 1, D_MODEL), F32)], axis=0)
    mods = _ada_call(s, ada_w, ada_b).reshape(DEPTH, MOD_ROWS, 9, D_MODEL)
    dft_lat, dft_ctx = _block_dft_tables(min(HY_S, SEQ)), _block_dft_tables(min(HY_S, CTX_LEN))

    for l in range(DEPTH):
        last = l == DEPTH - 1
        lg = lambda i: ln_g[l, i][None, :]
        lb = lambda i: ln_b[l, i][None, :]
        w_ffn = _ffn_weights(ffn_wg[l, 0], ffn_wu[l, 0], ffn_wd[l, 0])
        h_lat = _ffn_call(LAT, h_lat, mods, l, w_ffn, lg(0), lb(0))
        h_ctx = _ffn_call(CTX, h_ctx, mods, l, w_ffn, lg(0), lb(0))
        ffn2 = (_ffn_weights(ffn_wg[l, 1], ffn_wu[l, 1], ffn_wd[l, 1]), lg(2), lb(2))

        if l % 2 == 0:
            e = l // 2
            w_in = ev_w_in[e].astype(BF16)
            w_out = ev_w_out[e].astype(BF16)
            hy_params = (hy_conv_w[e], hy_conv_b[e][None, :], hy_f_w1[e], hy_f_b1[e], hy_f_freq1[e], hy_f_w2[e],
                         hy_f_b2[e], hy_f_freq2[e], hy_f_w3[e], hy_skip[e])
            rw_params = _rw_prep_params(rw_mu[e], rw_k_k[e], rw_k_a[e], rw_r_k[e], rw_w0[e], rw_a0[e],
                                        rw_w_up[e], rw_a_up[e], rw_g_up[e])
            zl_hy, zl_rw = _inproj_call(LAT, h_lat, mods, l, w_in, (HY_IN, RW_IN))
            zc_hy, zc_rw = _inproj_call(CTX, h_ctx, mods, l, w_in, (HY_IN, RW_IN))
            hy_lat = _hyena(zl_hy, SEQ, dft_lat, *hy_params)
            hy_ctx = _hyena(_unpair_ctx(zc_hy), CTX_LEN, dft_ctx, *hy_params)
            prep_lat = _rw_prep_call(LAT_SEG, zl_rw, rw_params)
            prep_ctx = _rw_prep_call(CTX_SEG, _unpair_ctx(zc_rw), rw_params)
            o_ctx, o_lat = _rwkv_bidir(prep_ctx, prep_lat)
            h_lat = _even_out_call(LAT, h_lat, mods, l, hy_lat, o_lat, prep_lat[4], prep_lat[3],
                                   rw_gn_g[e], rw_gn_b[e], w_out, lg(1), lb(1), *ffn2)
            if not last:
                h_ctx = _even_out_call(CTX, h_ctx, mods, l, _pair_ctx(hy_ctx), _pair_ctx(o_ctx),
                                       _pair_ctx(prep_ctx[4]), _pair_ctx(prep_ctx[3]),
                                       rw_gn_g[e], rw_gn_b[e], w_out, lg(1), lb(1), *ffn2)
        else:
            o = l // 2
            w = od_w_in[o]
            n_qkv = 2 * GLA_DK + GLA_DV
            n_gate = 2 * GLA_GATE_LORA
            w_in = jnp.concatenate([w[:, :n_qkv], w[:, n_qkv + n_gate:], w[:, n_qkv:n_qkv + n_gate],
                                    jnp.zeros((D_MODEL, GLA_GPAD - n_gate), F32)], axis=1).astype(BF16)
            w_out = od_w_out[o].astype(BF16)
            pad = jnp.zeros((GLA_GPAD - n_gate, GLA_DK), F32)
            zero = jnp.zeros((GLA_GATE_LORA, GLA_DK), F32)
            gup = jnp.concatenate([jnp.concatenate([gla_g_up[o, 0], zero, pad], axis=0),
                                   jnp.concatenate([zero, gla_g_up[o, 1], pad], axis=0)], axis=1)
            g_bias = gla_g_b[o].reshape(1, 2 * GLA_DK)
            h_lat = _raster_to_columns(h_lat)
            ql, kl, vl, ogl, gfl, gbl = _odd_inproj_call(LAT, h_lat, mods, l, w_in, gup, g_bias)
            qc, kc, vc, ogc, gfc, gbc = [_unpair_ctx(a)
                                         for a in _odd_inproj_call(CTX, h_ctx, mods, l, w_in, gup, g_bias)]
            zero_state = jnp.zeros((BATCH, GLA_H, GLA_HV, GLA_HK), F32)
            oc_f, oc_b, s_f, s_b = _gla_call(qc, kc, vc, gfc, gbc, zero_state, zero_state)
            ol_f, ol_b, _, _ = _gla_call(ql, kl, vl, gfl, gbl, s_f, s_b)
            h_lat = _odd_out_call(LAT, h_lat, mods, l, ol_f, ol_b, ogl, gla_norm_g[o], w_out, lg(1), lb(1),
                                  *ffn2)
            h_lat = _columns_to_raster(h_lat)
            if not last:
                h_ctx = _odd_out_call(CTX, h_ctx, mods, l, _pair_ctx(oc_f), _pair_ctx(oc_b),
                                      _pair_ctx(ogc), gla_norm_g[o], w_out, lg(1), lb(1), *ffn2)
    return h_lat
```
